```python
import math
import jax, jax.numpy as jnp
from jax import lax
import numpy as np

D_MODEL = 1024
BATCH = 4
SEQ = 4096
DEPTH = 2
DEC_BATCH = 32
DEC_SEQ = 64
PAST_LEN = 4096

CHUNK = 64
Q_BLOCK = 128
N_AB_LAYERS = (DEPTH + 1) // 2
N_SB_LAYERS = DEPTH // 2
N_SUB = 3
DN_HEADS = 8
DN_DK = 64
DN_DV = 64
DN_WIDTH = DN_HEADS * DN_DK
DN_CONV = 4
SC_WIDTH = D_MODEL // 2
SC_CONV = 3
AB_PROJ = 3 * DN_WIDTH + DN_HEADS * DN_DV + 2 * DN_HEADS + 3 * SC_WIDTH
AB_OUT = DN_HEADS * DN_DV + SC_WIDTH
SB_HEADS = 16
SB_DH = D_MODEL // SB_HEADS
SB_WIDTH = SB_HEADS * SB_DH
D_FF = 2816
NORM_EPS = 1e-6

kernel_name = 'hybrid_gdn_shortconv_stickbreak_stream_step'


def _rmsnorm(x, g):
    xf = x.astype(jnp.float32)
    y = xf * lax.rsqrt(jnp.mean(xf * xf, axis=-1, keepdims=True) + NORM_EPS)
    return y * g.astype(jnp.float32)


def _modulated_norm(x, g, shift, scale):
    y = _rmsnorm(x, g) * (1.0 + scale[:, None].astype(jnp.float32)) + shift[:, None].astype(jnp.float32)
    return y.astype(x.dtype)


def _swiglu(h, w_in, w_out):
    gate, up = jnp.split(h @ w_in, 2, axis=-1)
    return (jax.nn.silu(gate) * up) @ w_out


def _l2norm(x):
    return x * lax.rsqrt(jnp.sum(x * x, axis=-1, keepdims=True) + 1e-6)


def _causal_dwconv(x, prev, w):
    width = w.shape[0]
    t = x.shape[1]
    xp = jnp.concatenate([prev.astype(x.dtype), x], axis=1)
    y = xp[:, 0:t] * w[0]
    for j in range(1, width):
        y = y + xp[:, j:j + t] * w[j]
    return y, xp[:, t:]


def _gdn_chunk(S, xs):
    q, k, v, g, beta = xs
    c = q.shape[2]
    incl = jnp.tril(jnp.ones((c, c), dtype=bool))
    strict = jnp.tril(jnp.ones((c, c), dtype=bool), k=-1)
    G = jnp.cumsum(g, axis=-1)
    decay = jnp.exp(jnp.where(incl, G[..., :, None] - G[..., None, :], -jnp.inf))
    a_mat = jnp.where(strict, beta[..., :, None] * jnp.einsum('bhtd,bhsd->bhts', k, k) * decay, 0.0)
    eye = jnp.eye(c, dtype=a_mat.dtype)
    rhs = jnp.concatenate([beta[..., None] * v, (beta * jnp.exp(G))[..., None] * k], axis=-1)
    sol = lax.linalg.triangular_solve(eye + a_mat, rhs, left_side=True, lower=True)
    dv = v.shape[-1]
    u = sol[..., :dv] - jnp.einsum('bhtk,bhkv->bhtv', sol[..., dv:], S)
    qk = jnp.einsum('bhtd,bhsd->bhts', q, k) * decay
    o = jnp.exp(G)[..., None] * jnp.einsum('bhtk,bhkv->bhtv', q, S) + jnp.einsum('bhts,bhsv->bhtv', qk, u)
    G_end = G[..., -1:]
    S_new = jnp.exp(G_end)[..., None] * S + jnp.einsum('bhtk,bhtv->bhkv', k * jnp.exp(G_end - G)[..., None], u)
    return S_new, o


def _gated_delta(q, k, v, g, beta, S0):
    b, h, t, _ = q.shape
    c = min(t, CHUNK)
    n = t // c

    def to_chunks(a):
        a = a.reshape((b, h, n, c) + a.shape[3:])
        return jnp.moveaxis(a, 2, 0)

    S, o = lax.scan(_gdn_chunk, S0, (to_chunks(q), to_chunks(k), to_chunks(v), to_chunks(g), to_chunks(beta)))
    o = jnp.moveaxis(o, 0, 2).reshape(b, h, t, -1)
    return o, S


def _ab_mixer(h, S0, conv_prev, sc_prev, w_in, w_conv, A_log, dt_bias, dn_g, w_sc, w_out):
    b, t, _ = h.shape
    f32 = jnp.float32
    splits = np.cumsum([3 * DN_WIDTH, DN_HEADS * DN_DV, DN_HEADS, DN_HEADS, SC_WIDTH, SC_WIDTH]).tolist()
    qkv, z, a, bt, sB, sC, sx = jnp.split(h @ w_in, splits, axis=-1)
    qkv, conv_state = _causal_dwconv(qkv, conv_prev, w_conv)
    qkv = jax.nn.silu(qkv).astype(f32)
    q, k, v = jnp.split(qkv, 3, axis=-1)

    def heads(x):
        return x.reshape(b, t, DN_HEADS, -1).transpose(0, 2, 1, 3)

    q = _l2norm(heads(q)) * DN_DK ** -0.5
    k = _l2norm(heads(k))
    v = heads(v)
    g = -jnp.exp(A_log.astype(f32)) * jax.nn.softplus(a.astype(f32) + dt_bias.astype(f32))
    beta = jax.nn.sigmoid(bt.astype(f32))
    o, S = _gated_delta(q, k, v, g.transpose(0, 2, 1), beta.transpose(0, 2, 1), S0.astype(f32))
    o = o.transpose(0, 2, 1, 3)
    o = _rmsnorm(o, dn_g) * jax.nn.silu(z.astype(f32).reshape(b, t, DN_HEADS, DN_DV))
    o = o.reshape(b, t, DN_HEADS * DN_DV).astype(h.dtype)
    yc, sc_state = _causal_dwconv(sC * sx, sc_prev, w_sc)
    y_sc = sB * yc
    y = jnp.concatenate([o, y_sc], axis=-1) @ w_out
    return y, S.astype(S0.dtype), conv_state, sc_state


def _sb_block(q, k, v, q_pos, k_pos):
    z = jnp.einsum('hqd,hkd->hqk', q.astype(jnp.float32), k.astype(jnp.float32)) * SB_DH ** -0.5
    visible = k_pos[None, :] < q_pos[:, None]
    log_rest = jnp.where(visible, -jax.nn.softplus(z), 0.0)
    later = lax.cumsum(log_rest, axis=2, reverse=True) - log_rest
    w = jnp.where(visible, jnp.exp(jax.nn.log_sigmoid(z) + later), 0.0)
    return jnp.einsum('hqk,hkd->hqd', w, v.astype(jnp.float32))


def _sb_mixer(h, ck, cv, w_qkv, w_out):
    b, t, _ = h.shape
    past = ck.shape[2]
    q, k, v = jnp.split(h @ w_qkv, 3, axis=-1)

    def heads(x):
        return x.reshape(b, t, SB_HEADS, SB_DH).transpose(0, 2, 1, 3)

    q, k, v = heads(q), heads(k), heads(v)
    k_all = jnp.concatenate([ck.astype(k.dtype), k], axis=2)
    v_all = jnp.concatenate([cv.astype(v.dtype), v], axis=2)
    outs = []
    for q0 in range(0, t, Q_BLOCK):
        q1 = min(q0 + Q_BLOCK, t)
        q_pos = past + jnp.arange(q0, q1)
        k_pos = jnp.arange(past + q1)
        ob = lax.map(lambda a: _sb_block(a[0], a[1], a[2], q_pos, k_pos),
                     (q[:, :, q0:q1], k_all[:, :, :past + q1], v_all[:, :, :past + q1]))
        outs.append(ob)
    o = jnp.concatenate(outs, axis=2).astype(h.dtype)
    y = o.transpose(0, 2, 1, 3).reshape(b, t, SB_WIDTH) @ w_out
    return y, k, v


def _trunk(x, c, s_delta, s_qkv, s_sc, ck, cv, norm_g, ada_w, ada_b, ff_w_in, ff_w_out,
           ab_w_in, ab_conv_qkv, dn_A_log, dn_dt_bias, dn_norm_g, sc_conv, ab_w_out,
           sb_w_qkv, sb_w_out, final_g):
    b = x.shape[0]
    new_delta, new_qkv, new_sc, new_k, new_v = [], [], [], [], []
    cond = jax.nn.silu(c)
    for l in range(DEPTH):
        mod = (cond @ ada_w[l] + ada_b[l]).reshape(b, N_SUB, 3, D_MODEL)
        shift, scale, gate = mod[:, :, 0], mod[:, :, 1], mod[:, :, 2]
        hh = _modulated_norm(x, norm_g[l, 0], shift[:, 0], scale[:, 0])
        x = x + 0.5 * gate[:, 0, None] * _swiglu(hh, ff_w_in[l, 0], ff_w_out[l, 0])
        hh = _modulated_norm(x, norm_g[l, 1], shift[:, 1], scale[:, 1])
        i = l // 2
        if l % 2 == 0:
            y, S, cq, cs = _ab_mixer(hh, s_delta[i], s_qkv[i], s_sc[i], ab_w_in[i], ab_conv_qkv[i],
                                     dn_A_log[i], dn_dt_bias[i], dn_norm_g[i], sc_conv[i], ab_w_out[i])
            new_delta.append(S)
            new_qkv.append(cq)
            new_sc.append(cs)
        else:
            y, kr, vr = _sb_mixer(hh, ck[i], cv[i], sb_w_qkv[i], sb_w_out[i])
            new_k.append(kr)
            new_v.append(vr)
        x = x + gate[:, 1, None] * y
        hh = _modulated_norm(x, norm_g[l, 2], shift[:, 2], scale[:, 2])
        x = x + 0.5 * gate[:, 2, None] * _swiglu(hh, ff_w_in[l, 1], ff_w_out[l, 1])
    y = _rmsnorm(x, final_g).astype(x.dtype)
    return y, jnp.stack(new_delta), jnp.stack(new_qkv), jnp.stack(new_sc), jnp.stack(new_k), jnp.stack(new_v)


def setup_inputs(seed: int = 0) -> dict:
    key = jax.random.key(seed)
    ks = jax.random.split(key, 24)
    f32 = jnp.float32

    def nrm(k, shape, s=1.0):
        return s * jax.random.normal(k, shape, f32)

    x_prompt = nrm(ks[0], (BATCH, SEQ, D_MODEL))
    x_sample = nrm(ks[1], (DEC_BATCH, DEC_SEQ, D_MODEL))
    c_prompt = nrm(ks[2], (BATCH, D_MODEL))
    c_sample = nrm(ks[3], (DEC_BATCH, D_MODEL))
    state_delta = nrm(ks[4], (N_AB_LAYERS, DEC_BATCH, DN_HEADS, DN_DK, DN_DV), 0.5)
    state_qkv_conv = nrm(ks[5], (N_AB_LAYERS, DEC_BATCH, DN_CONV - 1, 3 * DN_WIDTH))
    state_sconv = nrm(ks[6], (N_AB_LAYERS, DEC_BATCH, SC_CONV - 1, SC_WIDTH))
    cache_k = nrm(ks[7], (N_SB_LAYERS, DEC_BATCH, SB_HEADS, PAST_LEN, SB_DH))
    cache_v = nrm(ks[8], (N_SB_LAYERS, DEC_BATCH, SB_HEADS, PAST_LEN, SB_DH))
    norm_g = 1.0 + nrm(ks[9], (DEPTH, N_SUB, D_MODEL), 0.02)
    ada_w = nrm(ks[10], (DEPTH, D_MODEL, N_SUB * 3 * D_MODEL), 0.25 * D_MODEL ** -0.5)
    ada_b = nrm(ks[11], (DEPTH, N_SUB, 3, D_MODEL), 0.02).at[:, :, 2].add(1.0).reshape(DEPTH, N_SUB * 3 * D_MODEL)
    ff_w_in = nrm(ks[12], (DEPTH, 2, D_MODEL, 2 * D_FF), D_MODEL ** -0.5)
    ff_w_out = nrm(ks[13], (DEPTH, 2, D_FF, D_MODEL), D_FF ** -0.5)
    ab_w_in = nrm(ks[14], (N_AB_LAYERS, D_MODEL, AB_PROJ), D_MODEL ** -0.5)
    ab_conv_qkv = nrm(ks[15], (N_AB_LAYERS, DN_CONV, 3 * DN_WIDTH), DN_CONV ** -0.5)
    dn_A_log = jnp.log(jax.random.uniform(ks[16], (N_AB_LAYERS, DN_HEADS), f32, 1.0, 16.0))
    dt = jnp.exp(jax.random.uniform(ks[17], (N_AB_LAYERS, DN_HEADS), f32, math.log(1e-3), math.log(1e-1)))
    dn_dt_bias = dt + jnp.log(-jnp.expm1(-dt))
    dn_norm_g = 1.0 + nrm(ks[18], (N_AB_LAYERS, DN_DV), 0.02)
    sc_conv = nrm(ks[19], (N_AB_LAYERS, SC_CONV, SC_WIDTH), SC_CONV ** -0.5)
    ab_w_out = nrm(ks[20], (N_AB_LAYERS, AB_OUT, D_MODEL), AB_OUT ** -0.5)
    sb_w_qkv = nrm(ks[21], (N_SB_LAYERS, D_MODEL, 3 * SB_WIDTH), D_MODEL ** -0.5)
    sb_w_out = nrm(ks[22], (N_SB_LAYERS, SB_WIDTH, D_MODEL), SB_WIDTH ** -0.5)
    final_g = 1.0 + nrm(ks[23], (D_MODEL,), 0.02)
    return {'x_prompt': x_prompt, 'x_sample': x_sample, 'c_prompt': c_prompt, 'c_sample': c_sample,
            'state_delta': state_delta, 'state_qkv_conv': state_qkv_conv, 'state_sconv': state_sconv,
            'cache_k': cache_k, 'cache_v': cache_v, 'norm_g': norm_g, 'ada_w': ada_w, 'ada_b': ada_b,
            'ff_w_in': ff_w_in, 'ff_w_out': ff_w_out, 'ab_w_in': ab_w_in, 'ab_conv_qkv': ab_conv_qkv,
            'dn_A_log': dn_A_log, 'dn_dt_bias': dn_dt_bias, 'dn_norm_g': dn_norm_g, 'sc_conv': sc_conv,
            'ab_w_out': ab_w_out, 'sb_w_qkv': sb_w_qkv, 'sb_w_out': sb_w_out, 'final_g': final_g}


def reference(x_prompt, x_sample, c_prompt, c_sample, state_delta, state_qkv_conv, state_sconv,
              cache_k, cache_v, norm_g, ada_w, ada_b, ff_w_in, ff_w_out, ab_w_in, ab_conv_qkv,
              dn_A_log, dn_dt_bias, dn_norm_g, sc_conv, ab_w_out, sb_w_qkv, sb_w_out, final_g):
    bp = x_prompt.shape[0]
    dt_ = x_prompt.dtype
    p_delta0 = jnp.zeros((N_AB_LAYERS, bp, DN_HEADS, DN_DK, DN_DV), dt_)
    p_qkv0 = jnp.zeros((N_AB_LAYERS, bp, DN_CONV - 1, 3 * DN_WIDTH), dt_)
    p_sc0 = jnp.zeros((N_AB_LAYERS, bp, SC_CONV - 1, SC_WIDTH), dt_)
    p_kv0 = jnp.zeros((N_SB_LAYERS, bp, SB_HEADS, 0, SB_DH), dt_)
    y_prompt, p_delta, p_qkv, p_sc, p_k, p_v = _trunk(
        x_prompt, c_prompt, p_delta0, p_qkv0, p_sc0, p_kv0, p_kv0, norm_g, ada_w, ada_b, ff_w_in, ff_w_out,
        ab_w_in, ab_conv_qkv, dn_A_log, dn_dt_bias, dn_norm_g, sc_conv, ab_w_out, sb_w_qkv, sb_w_out, final_g)
    y_sample, s_delta, s_qkv, s_sc, s_k, s_v = _trunk(
        x_sample, c_sample, state_delta, state_qkv_conv, state_sconv, cache_k, cache_v, norm_g, ada_w, ada_b,
        ff_w_in, ff_w_out, ab_w_in, ab_conv_qkv, dn_A_log, dn_dt_bias, dn_norm_g, sc_conv, ab_w_out,
        sb_w_qkv, sb_w_out, final_g)
    return (y_prompt, y_sample, p_delta, p_qkv, p_sc, p_k, p_v, s_delta, s_qkv, s_sc, s_k, s_v)
```

```python
import functools

import jax
import jax.numpy as jnp
from jax import lax
from jax.experimental import pallas as pl
from jax.experimental.pallas import tpu as pltpu

F32 = jnp.float32
BF16 = jnp.bfloat16

NORM_EPS = 1e-6
L2_EPS = 1e-6
N_SUB = 3
CHUNK = 64
DN_DK = 64
DN_CONV = 4
SC_CONV = 3
SB_DH = 64
ROW_TILE = 512
FF_TILE = 256
SB_BLOCK = 256
SUBLANES = 8
BF16_ROWS = 16
LANES = 128
VMEM_LIMIT = 56 * 1024 * 1024


def _params(*semantics):
    return pltpu.CompilerParams(dimension_semantics=semantics, vmem_limit_bytes=VMEM_LIMIT)


def _resident(shape):
    zeros = (0,) * len(shape)
    return pl.BlockSpec(shape, lambda *_: zeros, pipeline_mode=pl.Buffered(1))


def _row_tiling(nseq, t):
    if t >= ROW_TILE:
        assert t % ROW_TILE == 0
        return 1, ROW_TILE
    g = min(nseq, ROW_TILE // t)
    assert nseq % g == 0
    return g, t


def _dot(a, b):
    return jnp.dot(a, b, preferred_element_type=F32)


def _dot_nt(a, b):
    return lax.dot_general(a, b, (((1,), (1,)), ((), ())), preferred_element_type=F32)


def _split(x):
    hi = x.astype(BF16)
    lo = (x - hi.astype(F32)).astype(BF16)
    return hi, lo


def _dot_x2(x, m_bf16):
    hi, lo = _split(x)
    return _dot(hi, m_bf16) + _dot(lo, m_bf16)


def _split3(x):
    hi = x.astype(BF16)
    r = x - hi.astype(F32)
    mid = r.astype(BF16)
    lo = (r - mid.astype(F32)).astype(BF16)
    return hi, mid, lo


def _dot_x3(x, m_bf16):
    hi, mid, lo = _split3(x)
    return _dot(hi, m_bf16) + (_dot(mid, m_bf16) + _dot(lo, m_bf16))


def _mdot_x3(m_bf16, x):
    hi, mid, lo = _split3(x)
    return _dot(m_bf16, hi) + (_dot(m_bf16, mid) + _dot(m_bf16, lo))


def _dot_f32(x, y):
    xh, xl = _split(x)
    yh, yl = _split(y)
    return _dot(xh, yh) + (_dot(xl, yh) + _dot(xh, yl))


def _sigmoid(x):
    return 1.0 / (1.0 + jnp.exp(-x))


def _silu(x):
    return x * _sigmoid(x)


def _softplus(x):
    return jnp.maximum(x, 0.0) + jnp.log(1.0 + jnp.exp(-jnp.abs(x)))


def _rmsnorm(x, g):
    ms = jnp.mean(x * x, axis=-1, keepdims=True)
    return x * lax.rsqrt(ms + NORM_EPS) * g


def _modnorm(x, g, shift, scale):
    return _rmsnorm(x, g) * (1.0 + scale) + shift


def _ada_kernel(c_ref, w_ref, b_ref, o_ref):
    cond = _silu(c_ref[...]).astype(BF16)
    o_ref[0] = _dot(cond, w_ref[0]) + b_ref[0]


def _ada_modulation(c_all, ada_w, ada_b):
    depth, d, n = ada_w.shape
    r = c_all.shape[0]
    tn = 1024 if n % 1024 == 0 else n
    return pl.pallas_call(
        _ada_kernel,
        grid=(depth, n // tn),
        in_specs=[
            pl.BlockSpec((r, d), lambda l, j: (0, 0)),
            pl.BlockSpec((1, d, tn), lambda l, j: (l, 0, j)),
            pl.BlockSpec((1, 1, tn), lambda l, j: (l, 0, j)),
        ],
        out_specs=pl.BlockSpec((1, r, tn), lambda l, j: (l, 0, j)),
        out_shape=jax.ShapeDtypeStruct((depth, r, n), F32),
        compiler_params=_params("parallel", "parallel"),
        name="ada_modulation",
    )(c_all, ada_w, ada_b.reshape(depth, 1, n))


def _ffn_kernel(x_ref, sh_ref, sc_ref, gt_ref, g_ref, win_ref, wout_ref, *rest, final):
    o_ref = rest[-1]
    x = x_ref[...]
    gg, tt, d = x.shape
    f = wout_ref.shape[0]
    h = _modnorm(x, g_ref[...], sh_ref[...], sc_ref[...]).astype(BF16).reshape(gg * tt, d)
    acc = jnp.zeros((gg * tt, d), F32)
    for j in range(f // FF_TILE):
        lo = j * FF_TILE
        gate = _dot(h, win_ref[:, lo:lo + FF_TILE])
        up = _dot(h, win_ref[:, f + lo:f + lo + FF_TILE])
        act = (_silu(gate) * up).astype(BF16)
        acc = acc + _dot(act, wout_ref[lo:lo + FF_TILE, :])
    y = x + 0.5 * gt_ref[...] * acc.reshape(gg, tt, d)
    if final:
        y = _rmsnorm(y, rest[0][...])
    o_ref[...] = y


def _ffn(x, mod, g, w_in, w_out, final_g=None):
    nseq, t, d = x.shape
    f = w_out.shape[0]
    assert f % FF_TILE == 0
    gg, tt = _row_tiling(nseq, t)
    shift, scale, gate = mod
    xspec = pl.BlockSpec((gg, tt, d), lambda i, j: (i, j, 0))
    mspec = pl.BlockSpec((gg, 1, d), lambda i, j: (i, 0, 0))
    in_specs = [xspec, mspec, mspec, mspec, _resident((1, d)), _resident((d, 2 * f)), _resident((f, d))]
    args = [x, shift, scale, gate, g.reshape(1, d), w_in, w_out]
    if final_g is not None:
        in_specs.append(_resident((1, d)))
        args.append(final_g.reshape(1, d))
    return pl.pallas_call(
        functools.partial(_ffn_kernel, final=final_g is not None),
        grid=(nseq // gg, t // tt),
        in_specs=in_specs,
        out_specs=xspec,
        out_shape=jax.ShapeDtypeStruct(x.shape, F32),
        compiler_params=_params("parallel", "parallel"),
        name="ffn_final" if final_g is not None else "ffn",
    )(*args)


def _proj_kernel(x_ref, sh_ref, sc_ref, g_ref, *refs):
    n_out = len(refs) // 2
    x = x_ref[...]
    gg, tt, d = x.shape
    h = _modnorm(x, g_ref[...], sh_ref[...], sc_ref[...]).astype(BF16).reshape(gg * tt, d)
    for w_ref, o_ref in zip(refs[:n_out], refs[n_out:]):
        o_ref[...] = _dot(h, w_ref[...]).reshape(o_ref.shape).astype(o_ref.dtype)


def _norm_proj(x, mod, g, weights, name):
    nseq, t, d = x.shape
    gg, tt = _row_tiling(nseq, t)
    shift, scale, _ = mod
    xspec = pl.BlockSpec((gg, tt, d), lambda i, j: (i, j, 0))
    mspec = pl.BlockSpec((gg, 1, d), lambda i, j: (i, 0, 0))
    return pl.pallas_call(
        _proj_kernel,
        grid=(nseq // gg, t // tt),
        in_specs=[xspec, mspec, mspec, _resident((1, d))] + [_resident(w.shape) for w in weights],
        out_specs=[pl.BlockSpec((gg, tt, w.shape[1]), lambda i, j: (i, j, 0)) for w in weights],
        out_shape=[jax.ShapeDtypeStruct((nseq, t, w.shape[1]), F32) for w in weights],
        compiler_params=_params("parallel", "parallel"),
        name=name,
    )(x, shift, scale, g.reshape(1, d), *weights)


def _sb_qkv_kernel(x_ref, sh_ref, sc_ref, g_ref, w_ref, q_ref, k_ref, v_ref):
    x = x_ref[...]
    gg, tt, d = x.shape
    heads = q_ref.shape[1]
    width = heads * SB_DH
    h = _modnorm(x, g_ref[...], sh_ref[...], sc_ref[...]).astype(BF16).reshape(gg * tt, d)
    for i, (o_ref, mult) in enumerate(((q_ref, SB_DH ** -0.5), (k_ref, 1.0), (v_ref, 1.0))):
        y = _dot(h, w_ref[:, i * width:(i + 1) * width]).reshape(gg, tt, width)
        for hd in range(heads):
            o_ref[:, hd] = (y[:, :, hd * SB_DH:(hd + 1) * SB_DH] * mult).astype(o_ref.dtype)


def _sb_qkv(x, mod, g, w_qkv):
    nseq, t, d = x.shape
    heads = w_qkv.shape[1] // (3 * SB_DH)
    gg, tt = _row_tiling(nseq, t)
    shift, scale, _ = mod
    xspec = pl.BlockSpec((gg, tt, d), lambda i, j: (i, j, 0))
    mspec = pl.BlockSpec((gg, 1, d), lambda i, j: (i, 0, 0))
    ospec = pl.BlockSpec((gg, heads, tt, SB_DH), lambda i, j: (i, 0, j, 0))
    oshape = jax.ShapeDtypeStruct((nseq, heads, t, SB_DH), F32)
    return pl.pallas_call(
        _sb_qkv_kernel,
        grid=(nseq // gg, t // tt),
        in_specs=[xspec, mspec, mspec, _resident((1, d)), _resident(w_qkv.shape)],
        out_specs=[ospec, ospec, ospec],
        out_shape=[oshape, oshape, oshape],
        compiler_params=_params("parallel", "parallel"),
        name="sb_qkv",
    )(x, shift, scale, g.reshape(1, d), w_qkv)


def _out_proj_kernel(a_ref, x_ref, gt_ref, w_ref, o_ref):
    gg, tt, d = x_ref.shape
    y = _dot(a_ref[...].reshape(gg * tt, a_ref.shape[2]), w_ref[...])
    o_ref[...] = x_ref[...] + gt_ref[...] * y.reshape(gg, tt, d)


def _out_proj(a, x, gate, w):
    nseq, t, d = x.shape
    k = a.shape[2]
    gg, tt = _row_tiling(nseq, t)
    xspec = pl.BlockSpec((gg, tt, d), lambda i, j: (i, j, 0))
    return pl.pallas_call(
        _out_proj_kernel,
        grid=(nseq // gg, t // tt),
        in_specs=[pl.BlockSpec((gg, tt, k), lambda i, j: (i, j, 0)), xspec,
                  pl.BlockSpec((gg, 1, d), lambda i, j: (i, 0, 0)), _resident(w.shape)],
        out_specs=xspec,
        out_shape=jax.ShapeDtypeStruct(x.shape, F32),
        compiler_params=_params("parallel", "parallel"),
        name="out_proj",
    )(a, x, gate, w)


def _shifted(prev8, x, j):
    if j == 0:
        return x
    full = jnp.concatenate([prev8, x], axis=0)
    return pltpu.roll(full, j, 0)[SUBLANES:]


def _gdn_kernel(qkv_ref, z_ref, ab_ref, s_ref, s0_ref, conv0_ref, sc0_ref,
                wconv_ref, alog_ref, dtb_ref, dng_ref, wsc_ref,
                y_ref, sout_ref, convout_ref, scout_ref,
                state_ref, convprev_ref, scprev_ref):
    c = pl.program_id(1)
    heads = state_ref.shape[0]
    width = heads * DN_DK
    scw = wsc_ref.shape[1]

    @pl.when(c == 0)
    def _():
        state_ref[...] = s0_ref[0]
        convprev_ref[...] = conv0_ref[0]
        scprev_ref[...] = sc0_ref[0]

    pre = qkv_ref[0]
    prev = convprev_ref[...]
    wc = wconv_ref[...]
    conv = _shifted(prev, pre, DN_CONV - 1) * wc[0:1]
    for j in range(1, DN_CONV):
        conv = conv + _shifted(prev, pre, DN_CONV - 1 - j) * wc[j:j + 1]
    convprev_ref[...] = pre[CHUNK - SUBLANES:]
    qkv = _silu(conv)
    q, k, v = qkv[:, :width], qkv[:, width:2 * width], qkv[:, 2 * width:]

    ri = lax.broadcasted_iota(jnp.int32, (width, width), 0) // DN_DK
    ci = lax.broadcasted_iota(jnp.int32, (width, width), 1) // DN_DK
    head_ones = (ri == ci).astype(BF16)
    tt = lax.broadcasted_iota(jnp.int32, (CHUNK, width), 0)
    ss = lax.broadcasted_iota(jnp.int32, (CHUNK, width), 1) % DN_DK
    incl_all = ss <= tt
    eye_all = ss == tt
    t2 = lax.broadcasted_iota(jnp.int32, (CHUNK, CHUNK), 0)
    s2 = lax.broadcasted_iota(jnp.int32, (CHUNK, CHUNK), 1)
    tril = (s2 <= t2).astype(BF16)
    strict = s2 < t2
    ones_cc = jnp.ones((CHUNK, CHUNK), BF16)
    li = lax.broadcasted_iota(jnp.int32, (LANES, width), 0)
    hi_ = lax.broadcasted_iota(jnp.int32, (LANES, width), 1) // DN_DK
    expand_g = (li == hi_).astype(BF16)
    expand_b = (li == hi_ + heads).astype(BF16)

    q = q * lax.rsqrt(_dot_x2(q * q, head_ones) + L2_EPS) * (DN_DK ** -0.5)
    k = k * lax.rsqrt(_dot_x2(k * k, head_ones) + L2_EPS)

    ab = ab_ref[0]
    g_log = -jnp.exp(alog_ref[...]) * _softplus(ab + dtb_ref[...])
    g_cum = _mdot_x3(tril, g_log)
    gexp = _dot_x3(g_cum, expand_g)
    beta = _dot_x2(_sigmoid(ab), expand_b)
    grow = _mdot_x3(ones_cc, jnp.where(eye_all, gexp, 0.0))
    decay = jnp.where(incl_all, jnp.exp(jnp.where(incl_all, gexp - grow, 0.0)), 0.0)
    eg = jnp.exp(gexp)
    g_end = gexp[CHUNK - 1:CHUNK]
    k_dec = k * jnp.exp(g_end - gexp)
    eg_end = jnp.exp(g_end)
    bv = beta * v
    bk = beta * eg * k
    qg = eg * q

    outs = []
    for h in range(heads):
        sl = slice(h * DN_DK, (h + 1) * DN_DK)
        kh = k[:, sl].astype(BF16)
        qh = q[:, sl].astype(BF16)
        dec = decay[:, sl]
        s_h = state_ref[h]
        s_bf = s_h.astype(BF16)
        a_neg = -jnp.where(strict, beta[:, sl] * _dot_nt(kh, kh) * dec, 0.0)
        x = jnp.concatenate([bv[:, sl], bk[:, sl]], axis=1)
        p = a_neg
        x = x + _dot_f32(p, x)
        for _ in range(CHUNK.bit_length() - 2):
            p = _dot_f32(p, p)
            x = x + _dot_f32(p, x)
        u = x[:, :DN_DK] - _dot(x[:, DN_DK:].astype(BF16), s_bf)
        u_bf = u.astype(BF16)
        qk = (_dot_nt(qh, kh) * dec).astype(BF16)
        o = _dot(qg[:, sl].astype(BF16), s_bf) + _dot(qk, u_bf)
        kd_t = lax.dot_general(k_dec[:, sl].astype(BF16), u_bf, (((0,), (0,)), ((), ())),
                               preferred_element_type=F32)
        state_ref[h] = eg_end[:, sl][:, 0:1] * s_h + kd_t
        outs.append(o)
    o_all = jnp.concatenate(outs, axis=1)

    ms = _dot_x2(o_all * o_all, head_ones) * (1.0 / DN_DK)
    o_all = o_all * lax.rsqrt(ms + NORM_EPS) * dng_ref[...] * _silu(z_ref[0])

    s_in = s_ref[0]
    s_b, s_c, s_x = s_in[:, :scw], s_in[:, scw:2 * scw], s_in[:, 2 * scw:]
    cx = s_c * s_x
    scprev = scprev_ref[...]
    wsc = wsc_ref[...]
    yc = _shifted(scprev, cx, SC_CONV - 1) * wsc[0:1]
    for j in range(1, SC_CONV):
        yc = yc + _shifted(scprev, cx, SC_CONV - 1 - j) * wsc[j:j + 1]
    scprev_ref[...] = cx[CHUNK - SUBLANES:]

    y_ref[0, :, :width] = o_all.astype(y_ref.dtype)
    y_ref[0, :, width:] = (s_b * yc).astype(y_ref.dtype)

    @pl.when(c == pl.num_programs(1) - 1)
    def _():
        sout_ref[0] = state_ref[...]
        convout_ref[0] = convprev_ref[...]
        scout_ref[0] = scprev_ref[...]


def _gdn(qkv, z, ab, s, s0, conv0, sc0, wconv, alog, dtb, dng, wsc):
    nseq, t, w3 = qkv.shape
    heads = s0.shape[1]
    width = heads * DN_DK
    scw = wsc.shape[1]
    assert t % CHUNK == 0
    row = lambda n: pl.BlockSpec((1, CHUNK, n), lambda b, c: (b, c, 0))
    per_seq = lambda shp: pl.BlockSpec((1,) + shp, lambda b, c: (b,) + (0,) * len(shp))
    return pl.pallas_call(
        _gdn_kernel,
        grid=(nseq, t // CHUNK),
        in_specs=[row(w3), row(width), row(LANES), row(3 * scw),
                  per_seq((heads, DN_DK, DN_DK)), per_seq((SUBLANES, w3)), per_seq((SUBLANES, scw)),
                  _resident(wconv.shape), _resident((1, LANES)), _resident((1, LANES)),
                  _resident((1, width)), _resident(wsc.shape)],
        out_specs=[row(width + scw), per_seq((heads, DN_DK, DN_DK)),
                   per_seq((SUBLANES, w3)), per_seq((SUBLANES, scw))],
        out_shape=[jax.ShapeDtypeStruct((nseq, t, width + scw), BF16),
                   jax.ShapeDtypeStruct((nseq, heads, DN_DK, DN_DK), F32),
                   jax.ShapeDtypeStruct((nseq, SUBLANES, w3), F32),
                   jax.ShapeDtypeStruct((nseq, SUBLANES, scw), F32)],
        scratch_shapes=[pltpu.VMEM((heads, DN_DK, DN_DK), F32),
                        pltpu.VMEM((SUBLANES, w3), F32),
                        pltpu.VMEM((SUBLANES, scw), F32)],
        compiler_params=_params("parallel", "arbitrary"),
        name="gdn_sconv",
    )(qkv, z, ab, s, s0, conv0, sc0, wconv, alog, dtb, dng, wsc)


def _sb_block(q_bf, k_bf, v_bf, carry, upper, mask):
    z = _dot_nt(q_bf, k_bf)
    soft = jnp.log(1.0 + jnp.exp(-jnp.abs(z)))
    log_rest = -(jnp.maximum(z, 0.0) + soft)
    log_sig = jnp.minimum(z, 0.0) - soft
    if mask is not None:
        log_rest = jnp.where(mask, log_rest, 0.0)
    later = _dot_x2(log_rest, upper)
    w = jnp.exp(log_sig + later + carry)
    if mask is not None:
        w = jnp.where(mask, w, 0.0)
    o = _dot(w.astype(BF16), v_bf)
    return o, carry + jnp.sum(log_rest, axis=-1, keepdims=True)


def _upper(n):
    j = lax.broadcasted_iota(jnp.int32, (n, n), 0)
    s = lax.broadcasted_iota(jnp.int32, (n, n), 1)
    return (j > s).astype(BF16), s < j


def _sb_prompt_kernel(q_ref, k_ref, v_ref, o_ref, kbf_ref, vbf_ref):
    hp = q_ref.shape[1]
    t = q_ref.shape[2]
    blk = min(SB_BLOCK, t)
    upper, causal = _upper(blk)
    for hh in range(hp):
        kbf_ref[...] = k_ref[0, hh].astype(BF16)
        vbf_ref[...] = v_ref[0, hh].astype(BF16)

        def q_block(qb, _):
            q0 = pl.multiple_of(qb * blk, blk)
            q_bf = q_ref[0, hh, pl.ds(q0, blk), :].astype(BF16)
            o, carry = _sb_block(q_bf, kbf_ref[pl.ds(q0, blk), :], vbf_ref[pl.ds(q0, blk), :],
                                 jnp.zeros((blk, 1), F32), upper, causal)

            def k_block(i, oc):
                k0 = pl.multiple_of((qb - 1 - i) * blk, blk)
                ob, cb = _sb_block(q_bf, kbf_ref[pl.ds(k0, blk), :], vbf_ref[pl.ds(k0, blk), :],
                                   oc[1], upper, None)
                return oc[0] + ob, cb

            o, _ = lax.fori_loop(0, qb, k_block, (o, carry))
            o_ref[0, pl.ds(q0, blk), hh * SB_DH:(hh + 1) * SB_DH] = o.astype(o_ref.dtype)
            return 0

        lax.fori_loop(0, t // blk, q_block, 0)


def _sb_prompt(q, k, v):
    b, heads, t, dh = q.shape
    hp = LANES // dh
    spec = pl.BlockSpec((1, hp, t, dh), lambda i, j: (i, j, 0, 0))
    return pl.pallas_call(
        _sb_prompt_kernel,
        grid=(b, heads // hp),
        in_specs=[spec, spec, spec],
        out_specs=pl.BlockSpec((1, t, hp * dh), lambda i, j: (i, 0, j)),
        out_shape=jax.ShapeDtypeStruct((b, t, heads * dh), BF16),
        scratch_shapes=[pltpu.VMEM((t, dh), BF16), pltpu.VMEM((t, dh), BF16)],
        compiler_params=_params("parallel", "parallel"),
        name="sb_prompt",
    )(q, k, v)


def _sb_sample_kernel(q_ref, k_ref, v_ref, ck_ref, cv_ref, o_ref):
    hp = q_ref.shape[1]
    t = q_ref.shape[2]
    past = ck_ref.shape[2]
    blk = min(SB_BLOCK, past)
    upper, _ = _upper(blk)
    upper_new, causal = _upper(t)
    for hh in range(hp):
        q_bf = q_ref[0, hh].astype(BF16)
        o, carry = _sb_block(q_bf, k_ref[0, hh].astype(BF16), v_ref[0, hh].astype(BF16),
                             jnp.zeros((t, 1), F32), upper_new, causal)

        def k_block(i, oc):
            k0 = pl.multiple_of(past - (i + 1) * blk, blk)
            ob, cb = _sb_block(q_bf, ck_ref[0, hh, pl.ds(k0, blk), :].astype(BF16),
                               cv_ref[0, hh, pl.ds(k0, blk), :].astype(BF16), oc[1], upper, None)
            return oc[0] + ob, cb

        o, _ = lax.fori_loop(0, past // blk, k_block, (o, carry))
        o_ref[0, :, hh * SB_DH:(hh + 1) * SB_DH] = o.astype(o_ref.dtype)


def _sb_sample(q, k, v, ck, cv):
    b, heads, t, dh = q.shape
    past = ck.shape[2]
    hp = LANES // dh
    assert past % min(SB_BLOCK, past) == 0
    spec = pl.BlockSpec((1, hp, t, dh), lambda i, j: (i, j, 0, 0))
    cspec = pl.BlockSpec((1, hp, past, dh), lambda i, j: (i, j, 0, 0))
    return pl.pallas_call(
        _sb_sample_kernel,
        grid=(b, heads // hp),
        in_specs=[spec, spec, spec, cspec, cspec],
        out_specs=pl.BlockSpec((1, t, hp * dh), lambda i, j: (i, 0, j)),
        out_shape=jax.ShapeDtypeStruct((b, t, heads * dh), BF16),
        compiler_params=_params("parallel", "parallel"),
        name="sb_sample",
    )(q, k, v, ck, cv)


def _pad_rows(state, rows):
    return jnp.pad(state, ((0, 0), (rows - state.shape[1], 0), (0, 0)))


def _trunk(x, mods, s_delta, s_qkv, s_sc, cache, w):
    x = _ffn(x, mods[0][0], w["norm_g"][0, 0], w["ff_in"][0, 0], w["ff_out"][0, 0])
    qkv, z, ab, s = _norm_proj(x, mods[0][1], w["norm_g"][0, 1],
                               [w["ab_qkv"], w["ab_z"], w["ab_ab"], w["ab_s"]], "ab_in_proj")
    y, new_delta, conv8, sc8 = _gdn(qkv, z, ab, s, s_delta, _pad_rows(s_qkv, SUBLANES),
                                    _pad_rows(s_sc, SUBLANES), w["ab_conv"], w["alog"], w["dtb"],
                                    w["dng"], w["sc_conv"])
    x = _out_proj(y, x, mods[0][1][2], w["ab_out"])
    x = _ffn(x, mods[0][2], w["norm_g"][0, 2], w["ff_in"][0, 1], w["ff_out"][0, 1])
    x = _ffn(x, mods[1][0], w["norm_g"][1, 0], w["ff_in"][1, 0], w["ff_out"][1, 0])
    q, k, v = _sb_qkv(x, mods[1][1], w["norm_g"][1, 1], w["sb_qkv"])
    o = _sb_prompt(q, k, v) if cache is None else _sb_sample(q, k, v, cache[0], cache[1])
    x = _out_proj(o, x, mods[1][1][2], w["sb_out"])
    y_out = _ffn(x, mods[1][2], w["norm_g"][1, 2], w["ff_in"][1, 1], w["ff_out"][1, 1], final_g=w["final_g"])
    new_qkv = conv8[:, SUBLANES - (DN_CONV - 1):]
    new_sc = sc8[:, SUBLANES - (SC_CONV - 1):]
    return y_out, new_delta[None], new_qkv[None], new_sc[None], k[None], v[None]


def kernel(x_prompt, x_sample, c_prompt, c_sample, state_delta, state_qkv_conv, state_sconv, cache_k, cache_v,
           norm_g, ada_w, ada_b, ff_w_in, ff_w_out, ab_w_in, ab_conv_qkv, dn_A_log, dn_dt_bias, dn_norm_g,
           sc_conv, ab_w_out, sb_w_qkv, sb_w_out, final_g):
    depth, _, d = norm_g.shape
    assert depth == 2 and ab_w_in.shape[0] == 1 and sb_w_qkv.shape[0] == 1
    bp, bs = x_prompt.shape[0], x_sample.shape[0]
    heads = dn_A_log.shape[1]
    width = heads * DN_DK
    scw = sc_conv.shape[2]
    assert 2 * heads <= LANES

    o_z, o_a, o_s = 3 * width, 4 * width, 4 * width + 2 * heads
    w_in = ab_w_in[0]
    pad_lane = lambda a: jnp.pad(a, ((0, 0), (0, LANES - a.shape[1])))
    w = {
        "norm_g": norm_g, "final_g": final_g,
        "ff_in": ff_w_in.astype(BF16), "ff_out": ff_w_out.astype(BF16),
        "ab_qkv": w_in[:, :o_z].astype(BF16), "ab_z": w_in[:, o_z:o_a].astype(BF16),
        "ab_ab": pad_lane(w_in[:, o_a:o_s]).astype(BF16), "ab_s": w_in[:, o_s:].astype(BF16),
        "ab_conv": ab_conv_qkv[0], "sc_conv": sc_conv[0],
        "alog": pad_lane(dn_A_log), "dtb": pad_lane(dn_dt_bias),
        "dng": jnp.tile(dn_norm_g[0], heads).reshape(1, width),
        "ab_out": ab_w_out[0].astype(BF16), "sb_qkv": sb_w_qkv[0].astype(BF16), "sb_out": sb_w_out[0].astype(BF16),
    }

    c_all = jnp.concatenate([c_prompt, c_sample], axis=0)
    rows = -(-(bp + bs) // BF16_ROWS) * BF16_ROWS
    c_all = jnp.pad(c_all, ((0, rows - (bp + bs)), (0, 0)))
    mod = _ada_modulation(c_all, ada_w.astype(BF16), ada_b)[:, :bp + bs]
    mod = mod.reshape(depth, bp + bs, N_SUB, 3, 1, d)

    def mods_for(lo, hi):
        return [[tuple(mod[l, lo:hi, s, i] for i in range(3)) for s in range(N_SUB)] for l in range(depth)]

    zeros = lambda shape: jnp.zeros(shape, x_prompt.dtype)
    out_p = _trunk(x_prompt, mods_for(0, bp), zeros((bp, heads, DN_DK, DN_DK)),
                   zeros((bp, DN_CONV - 1, 3 * width)), zeros((bp, SC_CONV - 1, scw)), None, w)
    out_s = _trunk(x_sample, mods_for(bp, bp + bs), state_delta[0], state_qkv_conv[0], state_sconv[0],
                   (cache_k[0], cache_v[0]), w)
    return (out_p[0], out_s[0]) + out_p[1:] + out_s[1:]
```

```python
import functools
import math

import jax
import jax.numpy as jnp
from jax import lax
from jax.experimental import pallas as pl
from jax.experimental.pallas import tpu as pltpu

F32 = jnp.float32
BF16 = jnp.bfloat16

NORM_EPS = 1e-6
L2_EPS = 1e-6
LOG2_E = math.log2(math.e)
N_SUB = 3
CHUNK = 64
DN_DK = 64
DN_CONV = 4
PRECISE_LEVELS = 4
SC_CONV = 3
SB_DH = 64
ROW_TILE = 512
FF_TILE = 256
SB_BLOCK = 256
SUBLANES = 8
BF16_ROWS = 16
LANES = 128
VMEM_LIMIT = 56 * 1024 * 1024


def _params(*semantics):
    return pltpu.CompilerParams(dimension_semantics=semantics, vmem_limit_bytes=VMEM_LIMIT)


def _resident(shape):
    zeros = (0,) * len(shape)
    return pl.BlockSpec(shape, lambda *_: zeros, pipeline_mode=pl.Buffered(1))


def _row_tiling(nseq, t):
    if t >= ROW_TILE:
        assert t % ROW_TILE == 0
        return 1, ROW_TILE
    g = min(nseq, ROW_TILE // t)
    assert nseq % g == 0
    return g, t


def _dot(a, b):
    return jnp.dot(a, b, preferred_element_type=F32)


def _dot_nt(a, b):
    return lax.dot_general(a, b, (((1,), (1,)), ((), ())), preferred_element_type=F32)


def _dot_tn(a, b):
    return lax.dot_general(a, b, (((0,), (0,)), ((), ())), preferred_element_type=F32)


def _split(x):
    hi = x.astype(BF16)
    lo = (x - hi.astype(F32)).astype(BF16)
    return hi, lo


def _dot_x2(x, m_bf16):
    hi, lo = _split(x)
    return _dot(hi, m_bf16) + _dot(lo, m_bf16)


def _split3(x):
    hi = x.astype(BF16)
    r = x - hi.astype(F32)
    mid = r.astype(BF16)
    lo = (r - mid.astype(F32)).astype(BF16)
    return hi, mid, lo


def _dot_x3(x, m_bf16):
    hi, mid, lo = _split3(x)
    return _dot(hi, m_bf16) + (_dot(mid, m_bf16) + _dot(lo, m_bf16))


def _mdot_x3(m_bf16, x):
    hi, mid, lo = _split3(x)
    return _dot(m_bf16, hi) + (_dot(m_bf16, mid) + _dot(m_bf16, lo))


def _dot_f32(x, y):
    xh = x.astype(BF16).astype(F32)
    yh, yl = _split(y)
    lhs = jnp.concatenate([xh, x - xh, xh], axis=1).astype(BF16)
    return _dot(lhs, jnp.concatenate([yh, yh, yl], axis=0))


def _sigmoid(x):
    return 1.0 / (1.0 + jnp.exp(-x))


def _silu(x):
    return x * _sigmoid(x)


def _softplus(x):
    return jnp.maximum(x, 0.0) + jnp.log(1.0 + jnp.exp(-jnp.abs(x)))


def _rmsnorm(x, g):
    ms = jnp.mean(x * x, axis=-1, keepdims=True)
    return x * lax.rsqrt(ms + NORM_EPS) * g


def _modnorm(x, g, shift, scale):
    return _rmsnorm(x, g) * (1.0 + scale) + shift


def _ada_kernel(c_ref, w_ref, b_ref, o_ref):
    cond = _silu(c_ref[...]).astype(BF16)
    o_ref[0] = _dot(cond, w_ref[0]) + b_ref[0]


def _ada_modulation(c_all, ada_w, ada_b):
    depth, d, n = ada_w.shape
    r = c_all.shape[0]
    tn = 1024 if n % 1024 == 0 else n
    return pl.pallas_call(
        _ada_kernel,
        grid=(depth, n // tn),
        in_specs=[
            pl.BlockSpec((r, d), lambda l, j: (0, 0)),
            pl.BlockSpec((1, d, tn), lambda l, j: (l, 0, j)),
            pl.BlockSpec((1, 1, tn), lambda l, j: (l, 0, j)),
        ],
        out_specs=pl.BlockSpec((1, r, tn), lambda l, j: (l, 0, j)),
        out_shape=jax.ShapeDtypeStruct((depth, r, n), F32),
        compiler_params=_params("parallel", "parallel"),
        name="ada_modulation",
    )(c_all, ada_w, ada_b.reshape(depth, 1, n))


def _ffn_kernel(x_ref, sh_ref, sc_ref, gt_ref, g_ref, win_ref, wout_ref, *rest, final):
    o_ref = rest[-1]
    x = x_ref[...]
    gg, tt, d = x.shape
    f = wout_ref.shape[0]
    h = _modnorm(x, g_ref[...], sh_ref[...], sc_ref[...]).astype(BF16).reshape(gg * tt, d)
    acc = jnp.zeros((gg * tt, d), F32)
    for j in range(f // FF_TILE):
        lo = j * FF_TILE
        gate = _dot(h, win_ref[:, lo:lo + FF_TILE])
        up = _dot(h, win_ref[:, f + lo:f + lo + FF_TILE])
        act = (_silu(gate) * up).astype(BF16)
        acc = acc + _dot(act, wout_ref[lo:lo + FF_TILE, :])
    y = x + 0.5 * gt_ref[...] * acc.reshape(gg, tt, d)
    if final:
        y = _rmsnorm(y, rest[0][...])
    o_ref[...] = y


def _ffn(x, mod, g, w_in, w_out, final_g=None):
    nseq, t, d = x.shape
    f = w_out.shape[0]
    assert f % FF_TILE == 0
    gg, tt = _row_tiling(nseq, t)
    shift, scale, gate = mod
    xspec = pl.BlockSpec((gg, tt, d), lambda i, j: (i, j, 0))
    mspec = pl.BlockSpec((gg, 1, d), lambda i, j: (i, 0, 0))
    in_specs = [xspec, mspec, mspec, mspec, _resident((1, d)), _resident((d, 2 * f)), _resident((f, d))]
    args = [x, shift, scale, gate, g.reshape(1, d), w_in, w_out]
    if final_g is not None:
        in_specs.append(_resident((1, d)))
        args.append(final_g.reshape(1, d))
    return pl.pallas_call(
        functools.partial(_ffn_kernel, final=final_g is not None),
        grid=(nseq // gg, t // tt),
        in_specs=in_specs,
        out_specs=xspec,
        out_shape=jax.ShapeDtypeStruct(x.shape, F32),
        compiler_params=_params("parallel", "parallel"),
        name="ffn_final" if final_g is not None else "ffn",
    )(*args)


def _proj_kernel(x_ref, sh_ref, sc_ref, g_ref, *refs):
    n_out = len(refs) // 2
    x = x_ref[...]
    gg, tt, d = x.shape
    h = _modnorm(x, g_ref[...], sh_ref[...], sc_ref[...]).astype(BF16).reshape(gg * tt, d)
    for w_ref, o_ref in zip(refs[:n_out], refs[n_out:]):
        o_ref[...] = _dot(h, w_ref[...]).reshape(o_ref.shape).astype(o_ref.dtype)


def _norm_proj(x, mod, g, weights, name):
    nseq, t, d = x.shape
    gg, tt = _row_tiling(nseq, t)
    shift, scale, _ = mod
    xspec = pl.BlockSpec((gg, tt, d), lambda i, j: (i, j, 0))
    mspec = pl.BlockSpec((gg, 1, d), lambda i, j: (i, 0, 0))
    return pl.pallas_call(
        _proj_kernel,
        grid=(nseq // gg, t // tt),
        in_specs=[xspec, mspec, mspec, _resident((1, d))] + [_resident(w.shape) for w in weights],
        out_specs=[pl.BlockSpec((gg, tt, w.shape[1]), lambda i, j: (i, j, 0)) for w in weights],
        out_shape=[jax.ShapeDtypeStruct((nseq, t, w.shape[1]), F32) for w in weights],
        compiler_params=_params("parallel", "parallel"),
        name=name,
    )(x, shift, scale, g.reshape(1, d), *weights)


def _sb_qkv_kernel(x_ref, sh_ref, sc_ref, g_ref, wq_ref, wkt_ref, wvt_ref, q_ref, kt_ref, vt_ref):
    x = x_ref[...]
    gg, tt, d = x.shape
    heads = kt_ref.shape[1]
    h = _modnorm(x, g_ref[...], sh_ref[...], sc_ref[...]).astype(BF16).reshape(gg * tt, d)
    q = _dot(h, wq_ref[...]) * (SB_DH ** -0.5 * LOG2_E)
    q_ref[...] = q.reshape(q_ref.shape).astype(q_ref.dtype)
    for w_ref, o_ref in ((wkt_ref, kt_ref), (wvt_ref, vt_ref)):
        yt = _dot_nt(w_ref[...], h)
        for s in range(gg):
            o_ref[s] = yt[:, s * tt:(s + 1) * tt].reshape(heads, SB_DH, tt)


def _sb_qkv(x, mod, g, w_q, w_kt, w_vt):
    nseq, t, d = x.shape
    width = w_q.shape[1]
    heads = width // SB_DH
    gg, tt = _row_tiling(nseq, t)
    shift, scale, _ = mod
    xspec = pl.BlockSpec((gg, tt, d), lambda i, j: (i, j, 0))
    mspec = pl.BlockSpec((gg, 1, d), lambda i, j: (i, 0, 0))
    tspec = pl.BlockSpec((gg, heads, SB_DH, tt), lambda i, j: (i, 0, 0, j))
    tshape = jax.ShapeDtypeStruct((nseq, heads, SB_DH, t), F32)
    return pl.pallas_call(
        _sb_qkv_kernel,
        grid=(nseq // gg, t // tt),
        in_specs=[xspec, mspec, mspec, _resident((1, d)),
                  _resident(w_q.shape), _resident(w_kt.shape), _resident(w_vt.shape)],
        out_specs=[pl.BlockSpec((gg, tt, width), lambda i, j: (i, j, 0)), tspec, tspec],
        out_shape=[jax.ShapeDtypeStruct((nseq, t, width), BF16), tshape, tshape],
        compiler_params=_params("parallel", "parallel"),
        name="sb_qkv",
    )(x, shift, scale, g.reshape(1, d), w_q, w_kt, w_vt)


def _out_proj_kernel(a_ref, x_ref, gt_ref, w_ref, o_ref):
    gg, tt, d = x_ref.shape
    y = _dot(a_ref[...].reshape(gg * tt, a_ref.shape[2]), w_ref[...])
    o_ref[...] = x_ref[...] + gt_ref[...] * y.reshape(gg, tt, d)


def _out_proj(a, x, gate, w):
    nseq, t, d = x.shape
    k = a.shape[2]
    gg, tt = _row_tiling(nseq, t)
    xspec = pl.BlockSpec((gg, tt, d), lambda i, j: (i, j, 0))
    return pl.pallas_call(
        _out_proj_kernel,
        grid=(nseq // gg, t // tt),
        in_specs=[pl.BlockSpec((gg, tt, k), lambda i, j: (i, j, 0)), xspec,
                  pl.BlockSpec((gg, 1, d), lambda i, j: (i, 0, 0)), _resident(w.shape)],
        out_specs=xspec,
        out_shape=jax.ShapeDtypeStruct(x.shape, F32),
        compiler_params=_params("parallel", "parallel"),
        name="out_proj",
    )(a, x, gate, w)


def _causal_conv(x, prev8, w):
    taps = w.shape[0]
    full = jnp.concatenate([prev8, x], axis=0)
    y = x * w[taps - 1:taps]
    for j in range(taps - 1):
        y = y + pltpu.roll(full, taps - 1 - j, 0)[SUBLANES:] * w[j:j + 1]
    return y


def _gdn_kernel(qkv_ref, z_ref, ab_ref, s_ref, s0_ref, conv0_ref, sc0_ref,
                wconv_ref, alog_ref, dtb_ref, dng_ref, wsc_ref,
                y_ref, sout_ref, convout_ref, scout_ref,
                state_ref, convprev_ref, scprev_ref, o_scr):
    step = pl.program_id(1)
    gg, tt, _ = qkv_ref.shape
    heads = state_ref.shape[1]
    width = heads * DN_DK
    scw = wsc_ref.shape[1]
    nc = tt // CHUNK
    nb = gg * nc
    rows = gg * tt

    @pl.when(step == 0)
    def _():
        state_ref[...] = s0_ref[...]
        convprev_ref[...] = conv0_ref[...]
        scprev_ref[...] = sc0_ref[...]

    def per_seq(fn):
        parts = [fn(s) for s in range(gg)]
        return parts[0] if gg == 1 else jnp.concatenate(parts, axis=0)

    wc = wconv_ref[...]
    wsc = wsc_ref[...]
    qkv = _silu(per_seq(lambda s: _causal_conv(qkv_ref[s], convprev_ref[s], wc)))
    cx_parts = [s_ref[s, :, scw:2 * scw] * s_ref[s, :, 2 * scw:] for s in range(gg)]
    y_sc = per_seq(lambda s: s_ref[s, :, :scw] * _causal_conv(cx_parts[s], scprev_ref[s], wsc))
    for s in range(gg):
        convprev_ref[s] = qkv_ref[s, tt - SUBLANES:, :]
        scprev_ref[s] = cx_parts[s][tt - SUBLANES:]
    q, k, v = qkv[:, :width], qkv[:, width:2 * width], qkv[:, 2 * width:]

    ri = lax.broadcasted_iota(jnp.int32, (width, width), 0) // DN_DK
    ci = lax.broadcasted_iota(jnp.int32, (width, width), 1) // DN_DK
    head_ones = (ri == ci).astype(BF16)
    fr = lax.broadcasted_iota(jnp.int32, (rows, width), 0) % CHUNK
    ss = lax.broadcasted_iota(jnp.int32, (rows, width), 1) % DN_DK
    incl_all = ss <= fr
    eye_all = ss == fr
    rr = lax.broadcasted_iota(jnp.int32, (rows, rows), 0)
    cc = lax.broadcasted_iota(jnp.int32, (rows, rows), 1)
    same_chunk = (rr // CHUNK) == (cc // CHUNK)
    chunk_ones = same_chunk.astype(BF16)
    chunk_tril = jnp.where(cc <= rr, chunk_ones, jnp.zeros_like(chunk_ones))
    t2 = lax.broadcasted_iota(jnp.int32, (CHUNK, CHUNK), 0)
    s2 = lax.broadcasted_iota(jnp.int32, (CHUNK, CHUNK), 1)
    strict = s2 < t2
    li = lax.broadcasted_iota(jnp.int32, (LANES, width), 0)
    lh = lax.broadcasted_iota(jnp.int32, (LANES, width), 1) // DN_DK
    expand_g = (li == lh).astype(BF16)
    expand_b = (li == lh + heads).astype(BF16)

    q = q * lax.rsqrt(_dot_x2(q * q, head_ones) + L2_EPS) * (DN_DK ** -0.5)
    k = k * lax.rsqrt(_dot_x2(k * k, head_ones) + L2_EPS)

    ab = per_seq(lambda s: ab_ref[s])
    g_log = -jnp.exp(alog_ref[...]) * _softplus(ab + dtb_ref[...])
    g_cum = _mdot_x3(chunk_tril, g_log)
    gexp = _dot_x3(g_cum, expand_g)
    beta = _dot_x2(_sigmoid(ab), expand_b)
    grow = _mdot_x3(chunk_ones, jnp.where(eye_all, gexp, 0.0))
    decay = jnp.where(incl_all, jnp.exp(jnp.where(incl_all, gexp - grow, 0.0)), 0.0)
    eg = jnp.exp(gexp)
    g3 = gexp.reshape(nb, CHUNK, width)
    g_end = jnp.broadcast_to(g3[:, CHUNK - 1:CHUNK, :], g3.shape).reshape(rows, width)
    k_dec = (k * jnp.exp(g_end - gexp)).astype(BF16)
    eg_end = jnp.exp(g_end)
    bv = beta * v
    bk = beta * eg * k
    qg = eg * q
    q_bf = q.astype(BF16)
    k_bf = k.astype(BF16)

    def blk(a, n, h):
        return a[n * CHUNK:(n + 1) * CHUNK, h * DN_DK:(h + 1) * DN_DK]

    chains = [(n, h) for n in range(nb) for h in range(heads)]
    kk = {c: _dot_nt(blk(k_bf, *c), blk(k_bf, *c)) for c in chains}
    p = {c: -jnp.where(strict, blk(beta, *c) * kk[c] * blk(decay, *c), 0.0) for c in chains}
    x = {c: jnp.concatenate([blk(bv, *c), blk(bk, *c)], axis=1) for c in chains}
    levels = CHUNK.bit_length() - 1
    for lvl in range(levels):
        last = lvl == levels - 1
        res = {}
        for c in chains:
            rhs = x[c] if last else jnp.concatenate([x[c], p[c]], axis=1)
            res[c] = (_dot_f32(p[c], rhs) if lvl < PRECISE_LEVELS
                      else _dot(p[c].astype(BF16), rhs.astype(BF16)))
        for c in chains:
            x[c] = x[c] + res[c][:, :2 * DN_DK]
            if not last:
                p[c] = res[c][:, 2 * DN_DK:]
    x_bf = {c: x[c].astype(BF16) for c in chains}
    cn = {c: _dot_tn(blk(k_dec, *c), x_bf[c]) for c in chains}
    qk = {c: (_dot_nt(blk(q_bf, *c), blk(k_bf, *c)) * blk(decay, *c)).astype(BF16) for c in chains}
    rp = {c: _dot(qk[c], x_bf[c]) for c in chains}
    pc = {c: jnp.concatenate([blk(qg, *c) - rp[c][:, DN_DK:], cn[c][:, DN_DK:]], axis=0).astype(BF16)
          for c in chains}
    for s in range(gg):
        state = [state_ref[s, h] for h in range(heads)]
        for i in range(nc):
            n = s * nc + i
            ps = [_dot(pc[(n, h)], state[h].astype(BF16)) for h in range(heads)]
            for h in range(heads):
                o_scr[n * CHUNK:(n + 1) * CHUNK, h * DN_DK:(h + 1) * DN_DK] = ps[h][:CHUNK] + rp[(n, h)][:, :DN_DK]
                state[h] = blk(eg_end, n, h) * state[h] - ps[h][CHUNK:] + cn[(n, h)][:, :DN_DK]
        for h in range(heads):
            state_ref[s, h] = state[h]

    o_all = o_scr[...]
    ms = _dot_x2(o_all * o_all, head_ones) * (1.0 / DN_DK)
    z = per_seq(lambda s: z_ref[s])
    o_all = o_all * lax.rsqrt(ms + NORM_EPS) * dng_ref[...] * _silu(z)
    y_ref[:, :, :width] = o_all.reshape(gg, tt, width).astype(y_ref.dtype)
    y_ref[:, :, width:] = y_sc.reshape(gg, tt, scw).astype(y_ref.dtype)

    @pl.when(step == pl.num_programs(1) - 1)
    def _():
        sout_ref[...] = state_ref[...]
        convout_ref[...] = convprev_ref[...]
        scout_ref[...] = scprev_ref[...]


def _gdn(qkv, z, ab, s, s0, conv0, sc0, wconv, alog, dtb, dng, wsc):
    nseq, t, w3 = qkv.shape
    heads = s0.shape[1]
    width = heads * DN_DK
    scw = wsc.shape[1]
    gg, tt = _row_tiling(nseq, t)
    assert tt % CHUNK == 0
    row = lambda n: pl.BlockSpec((gg, tt, n), lambda i, j: (i, j, 0))
    per_seq = lambda shp: pl.BlockSpec((gg,) + shp, lambda i, j: (i,) + (0,) * len(shp))
    return pl.pallas_call(
        _gdn_kernel,
        grid=(nseq // gg, t // tt),
        in_specs=[row(w3), row(width), row(LANES), row(3 * scw),
                  per_seq((heads, DN_DK, DN_DK)), per_seq((SUBLANES, w3)), per_seq((SUBLANES, scw)),
                  _resident(wconv.shape), _resident((1, LANES)), _resident((1, LANES)),
                  _resident((1, width)), _resident(wsc.shape)],
        out_specs=[row(width + scw), per_seq((heads, DN_DK, DN_DK)),
                   per_seq((SUBLANES, w3)), per_seq((SUBLANES, scw))],
        out_shape=[jax.ShapeDtypeStruct((nseq, t, width + scw), BF16),
                   jax.ShapeDtypeStruct((nseq, heads, DN_DK, DN_DK), F32),
                   jax.ShapeDtypeStruct((nseq, SUBLANES, w3), F32),
                   jax.ShapeDtypeStruct((nseq, SUBLANES, scw), F32)],
        scratch_shapes=[pltpu.VMEM((gg, heads, DN_DK, DN_DK), F32),
                        pltpu.VMEM((gg, SUBLANES, w3), F32),
                        pltpu.VMEM((gg, SUBLANES, scw), F32),
                        pltpu.VMEM((gg * tt, width), F32)],
        compiler_params=_params("parallel", "arbitrary"),
        name="gdn_sconv",
    )(qkv, z, ab, s, s0, conv0, sc0, wconv, alog, dtb, dng, wsc)


def _sb_blocks(items, upper):
    def run(carries):
        zs = [_dot(q, kt) for _, q, kt, _, _ in items]
        log_sig, log_rest = [], []
        for z, (_, _, _, _, mask) in zip(zs, items):
            soft = jnp.log2(1.0 + jnp.exp2(-jnp.abs(z)))
            ls = jnp.minimum(z, 0.0) - soft
            lr = ls - z
            if mask is not None:
                lr = jnp.where(mask, lr, 0.0)
            log_sig.append(ls)
            log_rest.append(lr)
        laters = [_dot(lr.astype(BF16), upper[:lr.shape[1], :lr.shape[1]]) for lr in log_rest]
        out = {}
        ws = []
        for (chain, _, _, _, mask), ls, lr, later in zip(items, log_sig, log_rest, laters):
            carry = out[chain][1] if chain in out else carries[chain]
            w = jnp.exp2(ls + later + carry)
            if mask is not None:
                w = jnp.where(mask, w, 0.0)
            ws.append(w.astype(BF16))
            out[chain] = (out[chain][0] if chain in out else None, carry + jnp.sum(lr, axis=-1, keepdims=True))
        for (chain, _, _, vt, _), w in zip(items, ws):
            o = _dot_nt(w, vt)
            acc, carry = out[chain]
            out[chain] = (o if acc is None else acc + o, carry)
        return out
    return run


def _upper(n):
    j = lax.broadcasted_iota(jnp.int32, (n, n), 0)
    s = lax.broadcasted_iota(jnp.int32, (n, n), 1)
    return (j > s).astype(BF16)


def _sb_prompt_kernel(q_ref, kt_ref, vt_ref, o_ref, kbf_ref, vbf_ref):
    hp = kt_ref.shape[1]
    t = q_ref.shape[1]
    blk = min(SB_BLOCK, t)
    upper = _upper(blk)
    tq = lax.broadcasted_iota(jnp.int32, (blk, blk), 0)
    ts = lax.broadcasted_iota(jnp.int32, (blk, blk), 1)
    causal = ts < tq
    kbf_ref[...] = kt_ref[0].astype(BF16)
    vbf_ref[...] = vt_ref[0].astype(BF16)
    zero = jnp.zeros((blk, 1), F32)

    def q_block(qb, _):
        q0 = pl.multiple_of(qb * blk, blk)
        qs = [q_ref[0, pl.ds(q0, blk), hh * SB_DH:(hh + 1) * SB_DH] for hh in range(hp)]

        def items(k0, mask):
            return [(hh, qs[hh], kbf_ref[hh, :, pl.ds(k0, blk)], vbf_ref[hh, :, pl.ds(k0, blk)], mask)
                    for hh in range(hp)]

        first = _sb_blocks(items(q0, causal), upper)({hh: zero for hh in range(hp)})

        def k_block(i, oc):
            k0 = pl.multiple_of((qb - 1 - i) * blk, blk)
            res = _sb_blocks(items(k0, None), upper)({hh: oc[hh][1] for hh in range(hp)})
            return tuple((oc[hh][0] + res[hh][0], res[hh][1]) for hh in range(hp))

        done = lax.fori_loop(0, qb, k_block, tuple(first[hh] for hh in range(hp)))
        o = jnp.concatenate([done[hh][0] for hh in range(hp)], axis=1)
        o_ref[0, pl.ds(q0, blk), :] = o.astype(o_ref.dtype)
        return 0

    lax.fori_loop(0, t // blk, q_block, 0)


def _sb_prompt(q, kt, vt):
    b, heads, dh, t = kt.shape
    hp = LANES // dh
    assert t % min(SB_BLOCK, t) == 0
    qspec = pl.BlockSpec((1, t, hp * dh), lambda i, j: (i, 0, j))
    tspec = pl.BlockSpec((1, hp, dh, t), lambda i, j: (i, j, 0, 0))
    return pl.pallas_call(
        _sb_prompt_kernel,
        grid=(b, heads // hp),
        in_specs=[qspec, tspec, tspec],
        out_specs=qspec,
        out_shape=jax.ShapeDtypeStruct((b, t, heads * dh), BF16),
        scratch_shapes=[pltpu.VMEM((hp, dh, t), BF16), pltpu.VMEM((hp, dh, t), BF16)],
        compiler_params=_params("parallel", "parallel"),
        name="sb_prompt",
    )(q, kt, vt)


def _sb_sample_kernel(q_ref, kt_ref, vt_ref, ckt_ref, cvt_ref, o_ref):
    hs = kt_ref.shape[1]
    t = q_ref.shape[1]
    past = ckt_ref.shape[3]
    blk = min(SB_BLOCK, past)
    rows = hs * t
    upper = _upper(max(blk, t))
    rh = lax.broadcasted_iota(jnp.int32, (rows, hs * SB_DH), 0) // t
    ch = lax.broadcasted_iota(jnp.int32, (rows, hs * SB_DH), 1) // SB_DH
    own = rh == ch
    q4 = q_ref[0]
    q_bd = jnp.where(own, jnp.concatenate([q4] * hs, axis=0), jnp.zeros((), q4.dtype))
    fq = lax.broadcasted_iota(jnp.int32, (rows, t), 0) % t
    fs = lax.broadcasted_iota(jnp.int32, (rows, t), 1)
    causal = fs < fq

    new = (0, q_bd, kt_ref[0].reshape(hs * SB_DH, t).astype(BF16),
           vt_ref[0].reshape(hs * SB_DH, t).astype(BF16), causal)
    first = _sb_blocks([new], upper)({0: jnp.zeros((rows, 1), F32)})[0]

    def cache_item(k0):
        return (0, q_bd, ckt_ref[0, :, :, pl.ds(k0, blk)].reshape(hs * SB_DH, blk).astype(BF16),
                cvt_ref[0, :, :, pl.ds(k0, blk)].reshape(hs * SB_DH, blk).astype(BF16), None)

    nblk = past // blk
    pair = 2 if nblk % 2 == 0 else 1

    def k_blocks(i, oc):
        k0 = pl.multiple_of(past - (i + 1) * pair * blk, blk)
        its = [cache_item(k0 + (pair - 1 - r) * blk) for r in range(pair)]
        res = _sb_blocks(its, upper)({0: oc[1]})[0]
        return oc[0] + res[0], res[1]

    o_full, _ = lax.fori_loop(0, nblk // pair, k_blocks, first)
    o_own = jnp.where(own, o_full, 0.0)
    o = o_own[:t]
    for i in range(1, hs):
        o = o + o_own[i * t:(i + 1) * t]
    o_ref[0] = o.astype(o_ref.dtype)


def _sb_sample(q, kt, vt, ckt, cvt):
    b, heads, dh, t = kt.shape
    past = ckt.shape[3]
    hs = SB_BLOCK // t
    assert heads % hs == 0 and past % min(SB_BLOCK, past) == 0
    qspec = pl.BlockSpec((1, t, hs * dh), lambda i, j: (i, 0, j))
    tspec = pl.BlockSpec((1, hs, dh, t), lambda i, j: (i, j, 0, 0))
    cspec = pl.BlockSpec((1, hs, dh, past), lambda i, j: (i, j, 0, 0))
    return pl.pallas_call(
        _sb_sample_kernel,
        grid=(b, heads // hs),
        in_specs=[qspec, tspec, tspec, cspec, cspec],
        out_specs=qspec,
        out_shape=jax.ShapeDtypeStruct((b, t, heads * dh), BF16),
        compiler_params=_params("parallel", "parallel"),
        name="sb_sample",
    )(q, kt, vt, ckt, cvt)


def _pad_rows(state, rows):
    return jnp.pad(state, ((0, 0), (rows - state.shape[1], 0), (0, 0)))


def _trunk(x, mods, s_delta, s_qkv, s_sc, cache_t, w):
    x = _ffn(x, mods[0][0], w["norm_g"][0, 0], w["ff_in"][0, 0], w["ff_out"][0, 0])
    qkv, z, ab, s = _norm_proj(x, mods[0][1], w["norm_g"][0, 1],
                               [w["ab_qkv"], w["ab_z"], w["ab_ab"], w["ab_s"]], "ab_in_proj")
    y, new_delta, conv8, sc8 = _gdn(qkv, z, ab, s, s_delta, _pad_rows(s_qkv, SUBLANES),
                                    _pad_rows(s_sc, SUBLANES), w["ab_conv"], w["alog"], w["dtb"],
                                    w["dng"], w["sc_conv"])
    x = _out_proj(y, x, mods[0][1][2], w["ab_out"])
    x = _ffn(x, mods[0][2], w["norm_g"][0, 2], w["ff_in"][0, 1], w["ff_out"][0, 1])
    x = _ffn(x, mods[1][0], w["norm_g"][1, 0], w["ff_in"][1, 0], w["ff_out"][1, 0])
    q, kt, vt = _sb_qkv(x, mods[1][1], w["norm_g"][1, 1], w["sb_q"], w["sb_kt"], w["sb_vt"])
    o = _sb_prompt(q, kt, vt) if cache_t is None else _sb_sample(q, kt, vt, cache_t[0], cache_t[1])
    x = _out_proj(o, x, mods[1][1][2], w["sb_out"])
    y_out = _ffn(x, mods[1][2], w["norm_g"][1, 2], w["ff_in"][1, 1], w["ff_out"][1, 1], final_g=w["final_g"])
    new_qkv = conv8[:, SUBLANES - (DN_CONV - 1):]
    new_sc = sc8[:, SUBLANES - (SC_CONV - 1):]
    new_k = jnp.swapaxes(kt, -1, -2)
    new_v = jnp.swapaxes(vt, -1, -2)
    return y_out, new_delta[None], new_qkv[None], new_sc[None], new_k[None], new_v[None]


def kernel(x_prompt, x_sample, c_prompt, c_sample, state_delta, state_qkv_conv, state_sconv, cache_k, cache_v,
           norm_g, ada_w, ada_b, ff_w_in, ff_w_out, ab_w_in, ab_conv_qkv, dn_A_log, dn_dt_bias, dn_norm_g,
           sc_conv, ab_w_out, sb_w_qkv, sb_w_out, final_g):
    depth, _, d = norm_g.shape
    assert depth == 2 and ab_w_in.shape[0] == 1 and sb_w_qkv.shape[0] == 1
    bp, bs = x_prompt.shape[0], x_sample.shape[0]
    heads = dn_A_log.shape[1]
    width = heads * DN_DK
    scw = sc_conv.shape[2]
    sbw = sb_w_qkv.shape[2] // 3
    assert 2 * heads <= LANES

    o_z, o_a, o_s = 3 * width, 4 * width, 4 * width + 2 * heads
    w_in = ab_w_in[0]
    w_sb = sb_w_qkv[0].astype(BF16)
    pad_lane = lambda a: jnp.pad(a, ((0, 0), (0, LANES - a.shape[1])))
    w = {
        "norm_g": norm_g, "final_g": final_g,
        "ff_in": ff_w_in.astype(BF16), "ff_out": ff_w_out.astype(BF16),
        "ab_qkv": w_in[:, :o_z].astype(BF16), "ab_z": w_in[:, o_z:o_a].astype(BF16),
        "ab_ab": pad_lane(w_in[:, o_a:o_s]).astype(BF16), "ab_s": w_in[:, o_s:].astype(BF16),
        "ab_conv": ab_conv_qkv[0], "sc_conv": sc_conv[0],
        "alog": pad_lane(dn_A_log), "dtb": pad_lane(dn_dt_bias),
        "dng": jnp.tile(dn_norm_g[0], heads).reshape(1, width),
        "ab_out": ab_w_out[0].astype(BF16),
        "sb_q": w_sb[:, :sbw], "sb_kt": w_sb[:, sbw:2 * sbw].T, "sb_vt": w_sb[:, 2 * sbw:].T,
        "sb_out": sb_w_out[0].astype(BF16),
    }

    c_all = jnp.concatenate([c_prompt, c_sample], axis=0)
    rows = -(-(bp + bs) // BF16_ROWS) * BF16_ROWS
    c_all = jnp.pad(c_all, ((0, rows - (bp + bs)), (0, 0)))
    mod = _ada_modulation(c_all, ada_w.astype(BF16), ada_b)[:, :bp + bs]
    mod = mod.reshape(depth, bp + bs, N_SUB, 3, 1, d)

    def mods_for(lo, hi):
        return [[tuple(mod[l, lo:hi, s, i] for i in range(3)) for s in range(N_SUB)] for l in range(depth)]

    zeros = lambda shape: jnp.zeros(shape, x_prompt.dtype)
    out_p = _trunk(x_prompt, mods_for(0, bp), zeros((bp, heads, DN_DK, DN_DK)),
                   zeros((bp, DN_CONV - 1, 3 * width)), zeros((bp, SC_CONV - 1, scw)), None, w)
    cache_t = (jnp.swapaxes(cache_k[0], -1, -2), jnp.swapaxes(cache_v[0], -1, -2))
    out_s = _trunk(x_sample, mods_for(bp, bp + bs), state_delta[0], state_qkv_conv[0], state_sconv[0], cache_t, w)
    return (out_p[0], out_s[0]) + out_p[1:] + out_s[1:]
```

```python
import functools
import math

import jax
import jax.numpy as jnp
from jax import lax
from jax.experimental import pallas as pl
from jax.experimental.pallas import tpu as pltpu

F32 = jnp.float32
BF16 = jnp.bfloat16

NORM_EPS = 1e-6
L2_EPS = 1e-6
LOG2_E = math.log2(math.e)
N_SUB = 3
CHUNK = 64
DN_DK = 64
DN_CONV = 4
PRECISE_LEVELS = 4
SC_CONV = 3
SB_DH = 64
ROW_TILE = 512
FF_TILE = 256
SB_BLOCK = 256
SB_CHAINS = 4
SB_CACHE_GROUP = 2
SB_DEAD_LOG2 = -160.0
SUBLANES = 8
BF16_ROWS = 16
LANES = 128
VMEM_LIMIT = 56 * 1024 * 1024


def _params(*semantics):
    return pltpu.CompilerParams(dimension_semantics=semantics, vmem_limit_bytes=VMEM_LIMIT)


def _resident(shape):
    zeros = (0,) * len(shape)
    return pl.BlockSpec(shape, lambda *_: zeros, pipeline_mode=pl.Buffered(1))


def _row_tiling(nseq, t):
    if t >= ROW_TILE:
        assert t % ROW_TILE == 0
        return 1, ROW_TILE
    g = min(nseq, ROW_TILE // t)
    assert nseq % g == 0
    return g, t


def _dot(a, b):
    return jnp.dot(a, b, preferred_element_type=F32)


def _dot_nt(a, b):
    return lax.dot_general(a, b, (((1,), (1,)), ((), ())), preferred_element_type=F32)


def _dot_tn(a, b):
    return lax.dot_general(a, b, (((0,), (0,)), ((), ())), preferred_element_type=F32)


def _split(x):
    hi = x.astype(BF16)
    lo = (x - hi.astype(F32)).astype(BF16)
    return hi, lo


def _dot_x2(x, m_bf16):
    hi, lo = _split(x)
    return _dot(hi, m_bf16) + _dot(lo, m_bf16)


def _split3(x):
    hi = x.astype(BF16)
    r = x - hi.astype(F32)
    mid = r.astype(BF16)
    lo = (r - mid.astype(F32)).astype(BF16)
    return hi, mid, lo


def _dot_x3(x, m_bf16):
    hi, mid, lo = _split3(x)
    return _dot(hi, m_bf16) + (_dot(mid, m_bf16) + _dot(lo, m_bf16))


def _mdot_x3(m_bf16, x):
    hi, mid, lo = _split3(x)
    return _dot(m_bf16, hi) + (_dot(m_bf16, mid) + _dot(m_bf16, lo))


def _dot_f32(x, y):
    xh = x.astype(BF16).astype(F32)
    yh, yl = _split(y)
    lhs = jnp.concatenate([xh, x - xh, xh], axis=1).astype(BF16)
    return _dot(lhs, jnp.concatenate([yh, yh, yl], axis=0))


def _sigmoid(x):
    return 1.0 / (1.0 + jnp.exp(-x))


def _silu(x):
    return x * _sigmoid(x)


def _softplus(x):
    return jnp.maximum(x, 0.0) + jnp.log(1.0 + jnp.exp(-jnp.abs(x)))


def _rmsnorm(x, g):
    ms = jnp.mean(x * x, axis=-1, keepdims=True)
    return x * lax.rsqrt(ms + NORM_EPS) * g


def _modnorm(x, g, shift, scale):
    return _rmsnorm(x, g) * (1.0 + scale) + shift


def _ada_kernel(c_ref, w_ref, b_ref, o_ref):
    cond = _silu(c_ref[...]).astype(BF16)
    o_ref[0] = _dot(cond, w_ref[0]) + b_ref[0]


def _ada_modulation(c_all, ada_w, ada_b):
    depth, d, n = ada_w.shape
    r = c_all.shape[0]
    tn = 1024 if n % 1024 == 0 else n
    return pl.pallas_call(
        _ada_kernel,
        grid=(depth, n // tn),
        in_specs=[
            pl.BlockSpec((r, d), lambda l, j: (0, 0)),
            pl.BlockSpec((1, d, tn), lambda l, j: (l, 0, j)),
            pl.BlockSpec((1, 1, tn), lambda l, j: (l, 0, j)),
        ],
        out_specs=pl.BlockSpec((1, r, tn), lambda l, j: (l, 0, j)),
        out_shape=jax.ShapeDtypeStruct((depth, r, n), F32),
        compiler_params=_params("parallel", "parallel"),
        name="ada_modulation",
    )(c_all, ada_w, ada_b.reshape(depth, 1, n))


def _ffn_kernel(x_ref, sh_ref, sc_ref, gt_ref, g_ref, win_ref, wout_ref, *rest, mixer, final):
    o_ref = rest[-1]
    x = x_ref[...]
    gg, tt, d = x.shape
    f = wout_ref.shape[0]
    if mixer:
        a_ref, mg_ref, wo_ref = rest[:3]
        rest = rest[3:]
        mix = _dot(a_ref[...].reshape(gg * tt, a_ref.shape[2]), wo_ref[...])
        x = x + mg_ref[...] * mix.reshape(gg, tt, d)
    h = _modnorm(x, g_ref[...], sh_ref[...], sc_ref[...]).astype(BF16).reshape(gg * tt, d)
    acc = jnp.zeros((gg * tt, d), F32)
    for j in range(f // FF_TILE):
        lo = j * FF_TILE
        gate = _dot(h, win_ref[:, lo:lo + FF_TILE])
        up = _dot(h, win_ref[:, f + lo:f + lo + FF_TILE])
        act = (_silu(gate) * up).astype(BF16)
        acc = acc + _dot(act, wout_ref[lo:lo + FF_TILE, :])
    y = x + 0.5 * gt_ref[...] * acc.reshape(gg, tt, d)
    if final:
        y = _rmsnorm(y, rest[0][...])
    o_ref[...] = y


def _ffn(x, mod, g, w_in, w_out, mixer=None, final_g=None):
    nseq, t, d = x.shape
    f = w_out.shape[0]
    assert f % FF_TILE == 0
    gg, tt = _row_tiling(nseq, t)
    shift, scale, gate = mod
    xspec = pl.BlockSpec((gg, tt, d), lambda i, j: (i, j, 0))
    mspec = pl.BlockSpec((gg, 1, d), lambda i, j: (i, 0, 0))
    in_specs = [xspec, mspec, mspec, mspec, _resident((1, d)), _resident((d, 2 * f)), _resident((f, d))]
    args = [x, shift, scale, gate, g.reshape(1, d), w_in, w_out]
    if mixer is not None:
        a, mix_gate, w_o = mixer
        in_specs += [pl.BlockSpec((gg, tt, a.shape[2]), lambda i, j: (i, j, 0)), mspec, _resident(w_o.shape)]
        args += [a, mix_gate, w_o]
    if final_g is not None:
        in_specs.append(_resident((1, d)))
        args.append(final_g.reshape(1, d))
    return pl.pallas_call(
        functools.partial(_ffn_kernel, mixer=mixer is not None, final=final_g is not None),
        grid=(nseq // gg, t // tt),
        in_specs=in_specs,
        out_specs=xspec,
        out_shape=jax.ShapeDtypeStruct(x.shape, F32),
        compiler_params=_params("parallel", "parallel"),
        name="ffn_final" if final_g is not None else "ffn",
    )(*args)


def _proj_kernel(x_ref, sh_ref, sc_ref, g_ref, *refs):
    n_out = len(refs) // 2
    x = x_ref[...]
    gg, tt, d = x.shape
    h = _modnorm(x, g_ref[...], sh_ref[...], sc_ref[...]).astype(BF16).reshape(gg * tt, d)
    for w_ref, o_ref in zip(refs[:n_out], refs[n_out:]):
        o_ref[...] = _dot(h, w_ref[...]).reshape(o_ref.shape).astype(o_ref.dtype)


def _norm_proj(x, mod, g, weights, name):
    nseq, t, d = x.shape
    gg, tt = _row_tiling(nseq, t)
    shift, scale, _ = mod
    xspec = pl.BlockSpec((gg, tt, d), lambda i, j: (i, j, 0))
    mspec = pl.BlockSpec((gg, 1, d), lambda i, j: (i, 0, 0))
    return pl.pallas_call(
        _proj_kernel,
        grid=(nseq // gg, t // tt),
        in_specs=[xspec, mspec, mspec, _resident((1, d))] + [_resident(w.shape) for w in weights],
        out_specs=[pl.BlockSpec((gg, tt, w.shape[1]), lambda i, j: (i, j, 0)) for w in weights],
        out_shape=[jax.ShapeDtypeStruct((nseq, t, w.shape[1]), F32) for w in weights],
        compiler_params=_params("parallel", "parallel"),
        name=name,
    )(x, shift, scale, g.reshape(1, d), *weights)


def _sb_qkv_kernel(x_ref, sh_ref, sc_ref, g_ref, wq_ref, wkt_ref, wvt_ref,
                   q_ref, kt_ref, vt_ref, ktb_ref, vtb_ref):
    x = x_ref[...]
    gg, tt, d = x.shape
    heads = kt_ref.shape[1]
    h = _modnorm(x, g_ref[...], sh_ref[...], sc_ref[...]).astype(BF16).reshape(gg * tt, d)
    q = _dot(h, wq_ref[...]) * (SB_DH ** -0.5 * LOG2_E)
    q_ref[...] = q.reshape(q_ref.shape).astype(q_ref.dtype)
    for w_ref, o_ref, b_ref in ((wkt_ref, kt_ref, ktb_ref), (wvt_ref, vt_ref, vtb_ref)):
        yt = _dot_nt(w_ref[...], h)
        for s in range(gg):
            y = yt[:, s * tt:(s + 1) * tt].reshape(heads, SB_DH, tt)
            o_ref[s] = y
            b_ref[s] = y.astype(b_ref.dtype)


def _sb_qkv(x, mod, g, w_q, w_kt, w_vt):
    nseq, t, d = x.shape
    width = w_q.shape[1]
    heads = width // SB_DH
    gg, tt = _row_tiling(nseq, t)
    shift, scale, _ = mod
    xspec = pl.BlockSpec((gg, tt, d), lambda i, j: (i, j, 0))
    mspec = pl.BlockSpec((gg, 1, d), lambda i, j: (i, 0, 0))
    tspec = pl.BlockSpec((gg, heads, SB_DH, tt), lambda i, j: (i, 0, 0, j))
    tshape = jax.ShapeDtypeStruct((nseq, heads, SB_DH, t), F32)
    return pl.pallas_call(
        _sb_qkv_kernel,
        grid=(nseq // gg, t // tt),
        in_specs=[xspec, mspec, mspec, _resident((1, d)),
                  _resident(w_q.shape), _resident(w_kt.shape), _resident(w_vt.shape)],
        out_specs=[pl.BlockSpec((gg, tt, width), lambda i, j: (i, j, 0)), tspec, tspec, tspec, tspec],
        out_shape=[jax.ShapeDtypeStruct((nseq, t, width), BF16), tshape, tshape,
                   jax.ShapeDtypeStruct(tshape.shape, BF16), jax.ShapeDtypeStruct(tshape.shape, BF16)],
        compiler_params=_params("parallel", "parallel"),
        name="sb_qkv",
    )(x, shift, scale, g.reshape(1, d), w_q, w_kt, w_vt)


def _causal_conv(x, prev8, w):
    taps = w.shape[0]
    full = jnp.concatenate([prev8, x], axis=0)
    y = x * w[taps - 1:taps]
    for j in range(taps - 1):
        y = y + pltpu.roll(full, taps - 1 - j, 0)[SUBLANES:] * w[j:j + 1]
    return y


def _gdn_patterns(rows, heads):
    width = heads * DN_DK
    iota = lambda shape, dim: lax.broadcasted_iota(jnp.int32, shape, dim)
    head_ones = iota((width, width), 0) // DN_DK == iota((width, width), 1) // DN_DK
    rr, cc = iota((rows, rows), 0), iota((rows, rows), 1)
    chunk_ones = rr // CHUNK == cc // CHUNK
    chunk_tril = jnp.logical_and(chunk_ones, cc <= rr)
    li, lh = iota((LANES, width), 0), iota((LANES, width), 1) // DN_DK
    return [m.astype(BF16) for m in (head_ones, chunk_ones, chunk_tril, li == lh, li == lh + heads)]


def _gdn_kernel(qkv_ref, z_ref, ab_ref, s_ref, s0_ref, conv0_ref, sc0_ref,
                wconv_ref, alog_ref, dtb_ref, dng_ref, wsc_ref,
                head_ones_ref, chunk_ones_ref, chunk_tril_ref, expand_g_ref, expand_b_ref,
                y_ref, sout_ref, convout_ref, scout_ref,
                state_ref, convprev_ref, scprev_ref, o_scr):
    step = pl.program_id(1)
    gg, tt, _ = qkv_ref.shape
    heads = state_ref.shape[1]
    width = heads * DN_DK
    scw = wsc_ref.shape[1]
    nc = tt // CHUNK
    nb = gg * nc
    rows = gg * tt

    @pl.when(step == 0)
    def _():
        state_ref[...] = s0_ref[...]
        convprev_ref[...] = conv0_ref[...]
        scprev_ref[...] = sc0_ref[...]

    def per_seq(fn):
        parts = [fn(s) for s in range(gg)]
        return parts[0] if gg == 1 else jnp.concatenate(parts, axis=0)

    wc = wconv_ref[...]
    wsc = wsc_ref[...]
    qkv = _silu(per_seq(lambda s: _causal_conv(qkv_ref[s], convprev_ref[s], wc)))
    cx_parts = [s_ref[s, :, scw:2 * scw] * s_ref[s, :, 2 * scw:] for s in range(gg)]
    y_sc = per_seq(lambda s: s_ref[s, :, :scw] * _causal_conv(cx_parts[s], scprev_ref[s], wsc))
    for s in range(gg):
        convprev_ref[s] = qkv_ref[s, tt - SUBLANES:, :]
        scprev_ref[s] = cx_parts[s][tt - SUBLANES:]
    q, k, v = qkv[:, :width], qkv[:, width:2 * width], qkv[:, 2 * width:]

    head_ones = head_ones_ref[...]
    chunk_ones = chunk_ones_ref[...]
    chunk_tril = chunk_tril_ref[...]
    expand_g = expand_g_ref[...]
    expand_b = expand_b_ref[...]
    fr = lax.broadcasted_iota(jnp.int32, (rows, width), 0) % CHUNK
    ss = lax.broadcasted_iota(jnp.int32, (rows, width), 1) % DN_DK
    incl_all = ss <= fr
    eye_all = ss == fr
    t2 = lax.broadcasted_iota(jnp.int32, (CHUNK, CHUNK), 0)
    s2 = lax.broadcasted_iota(jnp.int32, (CHUNK, CHUNK), 1)
    strict = s2 < t2

    q = q * lax.rsqrt(_dot_x2(q * q, head_ones) + L2_EPS) * (DN_DK ** -0.5)
    k = k * lax.rsqrt(_dot_x2(k * k, head_ones) + L2_EPS)

    ab = per_seq(lambda s: ab_ref[s])
    g_log = -jnp.exp(alog_ref[...]) * _softplus(ab + dtb_ref[...])
    g_cum = _mdot_x3(chunk_tril, g_log)
    gexp = _dot_x3(g_cum, expand_g)
    beta = _dot_x2(_sigmoid(ab), expand_b)
    grow = _mdot_x3(chunk_ones, jnp.where(eye_all, gexp, 0.0))
    decay = jnp.where(incl_all, jnp.exp(jnp.where(incl_all, gexp - grow, 0.0)), 0.0)
    eg = jnp.exp(gexp)
    g3 = gexp.reshape(nb, CHUNK, width)
    g_end = jnp.broadcast_to(g3[:, CHUNK - 1:CHUNK, :], g3.shape).reshape(rows, width)
    k_dec = (k * jnp.exp(g_end - gexp)).astype(BF16)
    eg_end = jnp.exp(g_end)
    bv = beta * v
    bk = beta * eg * k
    qg = eg * q
    q_bf = q.astype(BF16)
    k_bf = k.astype(BF16)

    def blk(a, n, h):
        return a[n * CHUNK:(n + 1) * CHUNK, h * DN_DK:(h + 1) * DN_DK]

    chains = [(n, h) for n in range(nb) for h in range(heads)]
    kk = {c: _dot_nt(blk(k_bf, *c), blk(k_bf, *c)) for c in chains}
    p = {c: -jnp.where(strict, blk(beta, *c) * kk[c] * blk(decay, *c), 0.0) for c in chains}
    x = {c: jnp.concatenate([blk(bv, *c), blk(bk, *c)], axis=1) for c in chains}
    levels = CHUNK.bit_length() - 1
    for lvl in range(levels):
        last = lvl == levels - 1
        res = {}
        for c in chains:
            rhs = x[c] if last else jnp.concatenate([x[c], p[c]], axis=1)
            res[c] = (_dot_f32(p[c], rhs) if lvl < PRECISE_LEVELS
                      else _dot(p[c].astype(BF16), rhs.astype(BF16)))
        for c in chains:
            x[c] = x[c] + res[c][:, :2 * DN_DK]
            if not last:
                p[c] = res[c][:, 2 * DN_DK:]
    x_bf = {c: x[c].astype(BF16) for c in chains}
    cn = {c: _dot_tn(blk(k_dec, *c), x_bf[c]) for c in chains}
    qk = {c: (_dot_nt(blk(q_bf, *c), blk(k_bf, *c)) * blk(decay, *c)).astype(BF16) for c in chains}
    rp = {c: _dot(qk[c], x_bf[c]) for c in chains}
    pc = {c: jnp.concatenate([blk(qg, *c) - rp[c][:, DN_DK:], cn[c][:, DN_DK:]], axis=0).astype(BF16)
          for c in chains}
    for s in range(gg):
        state = [state_ref[s, h] for h in range(heads)]
        for i in range(nc):
            n = s * nc + i
            ps = [_dot(pc[(n, h)], state[h].astype(BF16)) for h in range(heads)]
            for h in range(heads):
                o_scr[n * CHUNK:(n + 1) * CHUNK, h * DN_DK:(h + 1) * DN_DK] = ps[h][:CHUNK] + rp[(n, h)][:, :DN_DK]
                state[h] = blk(eg_end, n, h) * state[h] - ps[h][CHUNK:] + cn[(n, h)][:, :DN_DK]
        for h in range(heads):
            state_ref[s, h] = state[h]

    o_all = o_scr[...]
    ms = _dot_x2(o_all * o_all, head_ones) * (1.0 / DN_DK)
    z = per_seq(lambda s: z_ref[s])
    o_all = o_all * lax.rsqrt(ms + NORM_EPS) * dng_ref[...] * _silu(z)
    y_ref[:, :, :width] = o_all.reshape(gg, tt, width).astype(y_ref.dtype)
    y_ref[:, :, width:] = y_sc.reshape(gg, tt, scw).astype(y_ref.dtype)

    @pl.when(step == pl.num_programs(1) - 1)
    def _():
        sout_ref[...] = state_ref[...]
        convout_ref[...] = convprev_ref[...]
        scout_ref[...] = scprev_ref[...]


def _gdn(qkv, z, ab, s, s0, conv0, sc0, wconv, alog, dtb, dng, wsc):
    nseq, t, w3 = qkv.shape
    heads = s0.shape[1]
    width = heads * DN_DK
    scw = wsc.shape[1]
    gg, tt = _row_tiling(nseq, t)
    assert tt % CHUNK == 0
    row = lambda n: pl.BlockSpec((gg, tt, n), lambda i, j: (i, j, 0))
    per_seq = lambda shp: pl.BlockSpec((gg,) + shp, lambda i, j: (i,) + (0,) * len(shp))
    patterns = _gdn_patterns(gg * tt, heads)
    return pl.pallas_call(
        _gdn_kernel,
        grid=(nseq // gg, t // tt),
        in_specs=[row(w3), row(width), row(LANES), row(3 * scw),
                  per_seq((heads, DN_DK, DN_DK)), per_seq((SUBLANES, w3)), per_seq((SUBLANES, scw)),
                  _resident(wconv.shape), _resident((1, LANES)), _resident((1, LANES)),
                  _resident((1, width)), _resident(wsc.shape)] + [_resident(m.shape) for m in patterns],
        out_specs=[row(width + scw), per_seq((heads, DN_DK, DN_DK)),
                   per_seq((SUBLANES, w3)), per_seq((SUBLANES, scw))],
        out_shape=[jax.ShapeDtypeStruct((nseq, t, width + scw), BF16),
                   jax.ShapeDtypeStruct((nseq, heads, DN_DK, DN_DK), F32),
                   jax.ShapeDtypeStruct((nseq, SUBLANES, w3), F32),
                   jax.ShapeDtypeStruct((nseq, SUBLANES, scw), F32)],
        scratch_shapes=[pltpu.VMEM((gg, heads, DN_DK, DN_DK), F32),
                        pltpu.VMEM((gg, SUBLANES, w3), F32),
                        pltpu.VMEM((gg, SUBLANES, scw), F32),
                        pltpu.VMEM((gg * tt, width), F32)],
        compiler_params=_params("parallel", "arbitrary"),
        name="gdn_sconv",
    )(qkv, z, ab, s, s0, conv0, sc0, wconv, alog, dtb, dng, wsc, *patterns)


def _sb_blocks(items, upper):
    def run(carries, scale_out):
        zs = [_dot(q, kt) for _, q, kt, _, _ in items]
        log_sig, log_rest = [], []
        for z, (_, _, _, _, mask) in zip(zs, items):
            soft = jnp.log2(1.0 + jnp.exp2(-jnp.abs(z)))
            ls = jnp.minimum(z, 0.0) - soft
            lr = ls - z
            if mask is not None:
                lr = jnp.where(mask, lr, 0.0)
            log_sig.append(ls)
            log_rest.append(lr)
        laters = [_dot(lr.astype(BF16), upper[:lr.shape[1], :lr.shape[1]]) for lr in log_rest]
        out = {}
        ws, scales = [], []
        for (chain, _, _, _, mask), ls, lr, later in zip(items, log_sig, log_rest, laters):
            carry = out[chain][1] if chain in out else carries[chain]
            w = jnp.exp2(ls + later if scale_out else ls + later + carry)
            if mask is not None:
                w = jnp.where(mask, w, 0.0)
            ws.append(w.astype(BF16))
            scales.append(jnp.exp2(carry) if scale_out else None)
            out[chain] = (out[chain][0] if chain in out else None, carry + (later[:, 0:1] + lr[:, 0:1]))
        for (chain, _, _, vt, _), w, scale in zip(items, ws, scales):
            o = _dot_nt(w, vt)
            if scale is not None:
                o = o * scale
            acc, carry = out[chain]
            out[chain] = (o if acc is None else acc + o, carry)
        return out
    return run


def _upper(n):
    j = lax.broadcasted_iota(jnp.int32, (n, n), 0)
    s = lax.broadcasted_iota(jnp.int32, (n, n), 1)
    return (j > s).astype(BF16)


def _any_alive(carries):
    top = functools.reduce(jnp.maximum, [jnp.max(c) for c in carries])
    return (top > SB_DEAD_LOG2).astype(jnp.int32)


def _sb_prompt_kernel(q_ref, kt_ref, vt_ref, o_ref):
    hp = kt_ref.shape[1]
    t = q_ref.shape[1]
    blk = min(SB_BLOCK, t)
    upper = _upper(blk)
    tq = lax.broadcasted_iota(jnp.int32, (blk, blk), 0)
    ts = lax.broadcasted_iota(jnp.int32, (blk, blk), 1)
    causal = ts < tq
    zero = jnp.zeros((blk, 1), F32)

    def q_block(qb, _):
        q0 = pl.multiple_of(qb * blk, blk)
        qs = [q_ref[0, pl.ds(q0, blk), hh * SB_DH:(hh + 1) * SB_DH] for hh in range(hp)]

        def items(k0, mask):
            return [(hh, qs[hh], kt_ref[0, hh, :, pl.ds(k0, blk)], vt_ref[0, hh, :, pl.ds(k0, blk)], mask)
                    for hh in range(hp)]

        first = _sb_blocks(items(q0, causal), upper)({hh: zero for hh in range(hp)}, True)

        def k_block(st):
            i, _, oc = st
            k0 = pl.multiple_of((qb - 1 - i) * blk, blk)
            res = _sb_blocks(items(k0, None), upper)({hh: oc[hh][1] for hh in range(hp)}, True)
            new = tuple((oc[hh][0] + res[hh][0], res[hh][1]) for hh in range(hp))
            return i + 1, _any_alive([c for _, c in new]), new

        start = tuple(first[hh] for hh in range(hp))
        done = lax.while_loop(lambda st: jnp.logical_and(st[0] < qb, st[1] > 0), k_block,
                              (jnp.int32(0), _any_alive([c for _, c in start]), start))[2]
        o = jnp.concatenate([done[hh][0] for hh in range(hp)], axis=1)
        o_ref[0, pl.ds(q0, blk), :] = o.astype(o_ref.dtype)
        return 0

    lax.fori_loop(0, t // blk, q_block, 0)


def _sb_prompt(q, kt, vt):
    b, heads, dh, t = kt.shape
    hp = min(SB_CHAINS, heads)
    assert t % min(SB_BLOCK, t) == 0 and heads % hp == 0 and (hp * dh) % LANES == 0
    qspec = pl.BlockSpec((1, t, hp * dh), lambda i, j: (i, 0, j))
    tspec = pl.BlockSpec((1, hp, dh, t), lambda i, j: (i, j, 0, 0))
    return pl.pallas_call(
        _sb_prompt_kernel,
        grid=(b, heads // hp),
        in_specs=[qspec, tspec, tspec],
        out_specs=qspec,
        out_shape=jax.ShapeDtypeStruct((b, t, heads * dh), BF16),
        compiler_params=_params("parallel", "parallel"),
        name="sb_prompt",
    )(q, kt, vt)


def _sb_sample_kernel(q_ref, kt_ref, vt_ref, ckt_ref, cvt_ref, o_ref):
    hs = kt_ref.shape[1]
    t = q_ref.shape[1]
    past = ckt_ref.shape[3]
    blk = min(SB_BLOCK, past)
    rows = hs * t
    upper = _upper(max(blk, t))
    rh = lax.broadcasted_iota(jnp.int32, (rows, hs * SB_DH), 0) // t
    ch = lax.broadcasted_iota(jnp.int32, (rows, hs * SB_DH), 1) // SB_DH
    own = rh == ch
    q4 = q_ref[0]
    q_bd = jnp.where(own, jnp.concatenate([q4] * hs, axis=0), jnp.zeros((), q4.dtype))
    fq = lax.broadcasted_iota(jnp.int32, (rows, t), 0) % t
    fs = lax.broadcasted_iota(jnp.int32, (rows, t), 1)
    causal = fs < fq

    new = (0, q_bd, kt_ref[0].reshape(hs * SB_DH, t), vt_ref[0].reshape(hs * SB_DH, t), causal)
    first = _sb_blocks([new], upper)({0: jnp.zeros((rows, 1), F32)}, False)[0]

    def cache_item(k0):
        return (0, q_bd, ckt_ref[0, :, :, pl.ds(k0, blk)].reshape(hs * SB_DH, blk).astype(BF16),
                cvt_ref[0, :, :, pl.ds(k0, blk)].reshape(hs * SB_DH, blk).astype(BF16), None)

    nblk = past // blk
    group = math.gcd(nblk, SB_CACHE_GROUP)

    def k_blocks(st):
        i, _, oc = st
        k0 = pl.multiple_of(past - (i + 1) * group * blk, blk)
        its = [cache_item(k0 + (group - 1 - r) * blk) for r in range(group)]
        res = _sb_blocks(its, upper)({0: oc[1]}, False)[0]
        return i + 1, _any_alive([res[1]]), (oc[0] + res[0], res[1])

    o_full, _ = lax.while_loop(lambda st: jnp.logical_and(st[0] < nblk // group, st[1] > 0), k_blocks,
                               (jnp.int32(0), _any_alive([first[1]]), first))[2]
    o_own = jnp.where(own, o_full, 0.0)
    o = o_own[:t]
    for i in range(1, hs):
        o = o + o_own[i * t:(i + 1) * t]
    o_ref[0] = o.astype(o_ref.dtype)


def _sb_sample(q, kt, vt, ckt, cvt):
    b, heads, dh, t = kt.shape
    past = ckt.shape[3]
    hs = SB_BLOCK // t
    assert heads % hs == 0 and past % min(SB_BLOCK, past) == 0
    qspec = pl.BlockSpec((1, t, hs * dh), lambda i, j: (i, 0, j))
    tspec = pl.BlockSpec((1, hs, dh, t), lambda i, j: (i, j, 0, 0))
    cspec = pl.BlockSpec((1, hs, dh, past), lambda i, j: (i, j, 0, 0))
    return pl.pallas_call(
        _sb_sample_kernel,
        grid=(b, heads // hs),
        in_specs=[qspec, tspec, tspec, cspec, cspec],
        out_specs=qspec,
        out_shape=jax.ShapeDtypeStruct((b, t, heads * dh), BF16),
        compiler_params=_params("parallel", "parallel"),
        name="sb_sample",
    )(q, kt, vt, ckt, cvt)


def _pad_rows(state, rows):
    return jnp.pad(state, ((0, 0), (rows - state.shape[1], 0), (0, 0)))


def _trunk(x, mods, s_delta, s_qkv, s_sc, cache_t, w):
    x = _ffn(x, mods[0][0], w["norm_g"][0, 0], w["ff_in"][0, 0], w["ff_out"][0, 0])
    qkv, z, ab, s = _norm_proj(x, mods[0][1], w["norm_g"][0, 1],
                               [w["ab_qkv"], w["ab_z"], w["ab_ab"], w["ab_s"]], "ab_in_proj")
    y, new_delta, conv8, sc8 = _gdn(qkv, z, ab, s, s_delta, _pad_rows(s_qkv, SUBLANES),
                                    _pad_rows(s_sc, SUBLANES), w["ab_conv"], w["alog"], w["dtb"],
                                    w["dng"], w["sc_conv"])
    x = _ffn(x, mods[0][2], w["norm_g"][0, 2], w["ff_in"][0, 1], w["ff_out"][0, 1],
             mixer=(y, mods[0][1][2], w["ab_out"]))
    x = _ffn(x, mods[1][0], w["norm_g"][1, 0], w["ff_in"][1, 0], w["ff_out"][1, 0])
    q, kt, vt, ktb, vtb = _sb_qkv(x, mods[1][1], w["norm_g"][1, 1], w["sb_q"], w["sb_kt"], w["sb_vt"])
    o = _sb_prompt(q, ktb, vtb) if cache_t is None else _sb_sample(q, ktb, vtb, cache_t[0], cache_t[1])
    y_out = _ffn(x, mods[1][2], w["norm_g"][1, 2], w["ff_in"][1, 1], w["ff_out"][1, 1],
                 mixer=(o, mods[1][1][2], w["sb_out"]), final_g=w["final_g"])
    new_qkv = conv8[:, SUBLANES - (DN_CONV - 1):]
    new_sc = sc8[:, SUBLANES - (SC_CONV - 1):]
    new_k = jnp.swapaxes(kt, -1, -2)
    new_v = jnp.swapaxes(vt, -1, -2)
    return y_out, new_delta[None], new_qkv[None], new_sc[None], new_k[None], new_v[None]


def kernel(x_prompt, x_sample, c_prompt, c_sample, state_delta, state_qkv_conv, state_sconv, cache_k, cache_v,
           norm_g, ada_w, ada_b, ff_w_in, ff_w_out, ab_w_in, ab_conv_qkv, dn_A_log, dn_dt_bias, dn_norm_g,
           sc_conv, ab_w_out, sb_w_qkv, sb_w_out, final_g):
    depth, _, d = norm_g.shape
    assert depth == 2 and ab_w_in.shape[0] == 1 and sb_w_qkv.shape[0] == 1
    bp, bs = x_prompt.shape[0], x_sample.shape[0]
    heads = dn_A_log.shape[1]
    width = heads * DN_DK
    scw = sc_conv.shape[2]
    sbw = sb_w_qkv.shape[2] // 3
    assert 2 * heads <= LANES

    o_z, o_a, o_s = 3 * width, 4 * width, 4 * width + 2 * heads
    w_in = ab_w_in[0]
    w_sb = sb_w_qkv[0].astype(BF16)
    pad_lane = lambda a: jnp.pad(a, ((0, 0), (0, LANES - a.shape[1])))
    w = {
        "norm_g": norm_g, "final_g": final_g,
        "ff_in": ff_w_in.astype(BF16), "ff_out": ff_w_out.astype(BF16),
        "ab_qkv": w_in[:, :o_z].astype(BF16), "ab_z": w_in[:, o_z:o_a].astype(BF16),
        "ab_ab": pad_lane(w_in[:, o_a:o_s]).astype(BF16), "ab_s": w_in[:, o_s:].astype(BF16),
        "ab_conv": ab_conv_qkv[0], "sc_conv": sc_conv[0],
        "alog": pad_lane(dn_A_log), "dtb": pad_lane(dn_dt_bias),
        "dng": jnp.tile(dn_norm_g[0], heads).reshape(1, width),
        "ab_out": ab_w_out[0].astype(BF16),
        "sb_q": w_sb[:, :sbw], "sb_kt": w_sb[:, sbw:2 * sbw].T, "sb_vt": w_sb[:, 2 * sbw:].T,
        "sb_out": sb_w_out[0].astype(BF16),
    }

    c_all = jnp.concatenate([c_prompt, c_sample], axis=0)
    rows = -(-(bp + bs) // BF16_ROWS) * BF16_ROWS
    c_all = jnp.pad(c_all, ((0, rows - (bp + bs)), (0, 0)))
    mod = _ada_modulation(c_all, ada_w.astype(BF16), ada_b)[:, :bp + bs]
    mod = mod.reshape(depth, bp + bs, N_SUB, 3, 1, d)

    def mods_for(lo, hi):
        return [[tuple(mod[l, lo:hi, s, i] for i in range(3)) for s in range(N_SUB)] for l in range(depth)]

    zeros = lambda shape: jnp.zeros(shape, x_prompt.dtype)
    out_p = _trunk(x_prompt, mods_for(0, bp), zeros((bp, heads, DN_DK, DN_DK)),
                   zeros((bp, DN_CONV - 1, 3 * width)), zeros((bp, SC_CONV - 1, scw)), None, w)
    cache_t = (jnp.swapaxes(cache_k[0], -1, -2), jnp.swapaxes(cache_v[0], -1, -2))
    out_s = _trunk(x_sample, mods_for(bp, bp + bs), state_delta[0], state_qkv_conv[0], state_sconv[0], cache_t, w)
    return (out_p[0], out_s[0]) + out_p[1:] + out_s[1:]
```

```python
import functools
import math

import jax
import jax.numpy as jnp
from jax import lax
from jax.experimental import pallas as pl
from jax.experimental.pallas import tpu as pltpu

F32 = jnp.float32
BF16 = jnp.bfloat16

NORM_EPS = 1e-6
L2_EPS = 1e-6
LOG2_E = math.log2(math.e)
N_SUB = 3
CHUNK = 64
DN_DK = 64
DN_CONV = 4
PRECISE_LEVELS = 4
SC_CONV = 3
SB_DH = 64
ROW_TILE = 512
FF_TILE = 256
SB_BLOCK = 256
SB_CHAINS = 4
SB_CACHE_GROUP = 2
SB_DEAD_LOG2 = -160.0
SUBLANES = 8
BF16_ROWS = 16
LANES = 128
VMEM_LIMIT = 56 * 1024 * 1024


def _params(*semantics):
    return pltpu.CompilerParams(dimension_semantics=semantics, vmem_limit_bytes=VMEM_LIMIT)


def _resident(shape):
    zeros = (0,) * len(shape)
    return pl.BlockSpec(shape, lambda *_: zeros, pipeline_mode=pl.Buffered(1))


def _row_tiling(nseq, t):
    if t >= ROW_TILE:
        assert t % ROW_TILE == 0
        return 1, ROW_TILE
    g = min(nseq, ROW_TILE // t)
    assert nseq % g == 0
    return g, t


def _dot(a, b):
    return jnp.dot(a, b, preferred_element_type=F32)


def _dot_nt(a, b):
    return lax.dot_general(a, b, (((1,), (1,)), ((), ())), preferred_element_type=F32)


def _dot_tn(a, b):
    return lax.dot_general(a, b, (((0,), (0,)), ((), ())), preferred_element_type=F32)


def _split(x):
    hi = x.astype(BF16)
    lo = (x - hi.astype(F32)).astype(BF16)
    return hi, lo


def _dot_x2(x, m_bf16):
    hi, lo = _split(x)
    return _dot(hi, m_bf16) + _dot(lo, m_bf16)


def _split3(x):
    hi = x.astype(BF16)
    r = x - hi.astype(F32)
    mid = r.astype(BF16)
    lo = (r - mid.astype(F32)).astype(BF16)
    return hi, mid, lo


def _dot_x3(x, m_bf16):
    hi, mid, lo = _split3(x)
    return _dot(hi, m_bf16) + (_dot(mid, m_bf16) + _dot(lo, m_bf16))


def _mdot_x3(m_bf16, x):
    hi, mid, lo = _split3(x)
    return _dot(m_bf16, hi) + (_dot(m_bf16, mid) + _dot(m_bf16, lo))


def _dot_f32(x, y):
    xh = x.astype(BF16).astype(F32)
    yh, yl = _split(y)
    lhs = jnp.concatenate([xh, x - xh, xh], axis=1).astype(BF16)
    return _dot(lhs, jnp.concatenate([yh, yh, yl], axis=0))


def _sigmoid(x):
    return 1.0 / (1.0 + jnp.exp(-x))


def _silu(x):
    return x * _sigmoid(x)


def _softplus(x):
    return jnp.maximum(x, 0.0) + jnp.log(1.0 + jnp.exp(-jnp.abs(x)))


def _rmsnorm(x, g):
    ms = jnp.mean(x * x, axis=-1, keepdims=True)
    return x * lax.rsqrt(ms + NORM_EPS) * g


def _modnorm(x, g, shift, scale):
    return _rmsnorm(x, g) * (1.0 + scale) + shift


def _ada_kernel(c_ref, w_ref, b_ref, o_ref):
    cond = _silu(c_ref[...]).astype(BF16)
    o_ref[0] = _dot(cond, w_ref[0]) + b_ref[0]


def _ada_modulation(c_all, ada_w, ada_b):
    depth, d, n = ada_w.shape
    r = c_all.shape[0]
    tn = 1024 if n % 1024 == 0 else n
    return pl.pallas_call(
        _ada_kernel,
        grid=(depth, n // tn),
        in_specs=[
            pl.BlockSpec((r, d), lambda l, j: (0, 0)),
            pl.BlockSpec((1, d, tn), lambda l, j: (l, 0, j)),
            pl.BlockSpec((1, 1, tn), lambda l, j: (l, 0, j)),
        ],
        out_specs=pl.BlockSpec((1, r, tn), lambda l, j: (l, 0, j)),
        out_shape=jax.ShapeDtypeStruct((depth, r, n), F32),
        compiler_params=_params("parallel", "parallel"),
        name="ada_modulation",
    )(c_all, ada_w, ada_b.reshape(depth, 1, n))


def _ffn_kernel(x_ref, sh_ref, sc_ref, gt_ref, g_ref, win_ref, wout_ref, *rest, mixer, final):
    o_ref = rest[-1]
    x = x_ref[...]
    gg, tt, d = x.shape
    f = wout_ref.shape[0]
    if mixer:
        a_ref, mg_ref, wo_ref = rest[:3]
        rest = rest[3:]
        mix = _dot(a_ref[...].reshape(gg * tt, a_ref.shape[2]), wo_ref[...])
        x = x + mg_ref[...] * mix.reshape(gg, tt, d)
    h = _modnorm(x, g_ref[...], sh_ref[...], sc_ref[...]).astype(BF16).reshape(gg * tt, d)
    acc = jnp.zeros((gg * tt, d), F32)
    for j in range(f // FF_TILE):
        lo = j * FF_TILE
        gate = _dot(h, win_ref[:, lo:lo + FF_TILE])
        up = _dot(h, win_ref[:, f + lo:f + lo + FF_TILE])
        act = (_silu(gate) * up).astype(BF16)
        acc = acc + _dot(act, wout_ref[lo:lo + FF_TILE, :])
    y = x + 0.5 * gt_ref[...] * acc.reshape(gg, tt, d)
    if final:
        y = _rmsnorm(y, rest[0][...])
    o_ref[...] = y


def _ffn(x, mod, g, w_in, w_out, mixer=None, final_g=None):
    nseq, t, d = x.shape
    f = w_out.shape[0]
    assert f % FF_TILE == 0
    gg, tt = _row_tiling(nseq, t)
    shift, scale, gate = mod
    xspec = pl.BlockSpec((gg, tt, d), lambda i, j: (i, j, 0))
    mspec = pl.BlockSpec((gg, 1, d), lambda i, j: (i, 0, 0))
    in_specs = [xspec, mspec, mspec, mspec, _resident((1, d)), _resident((d, 2 * f)), _resident((f, d))]
    args = [x, shift, scale, gate, g.reshape(1, d), w_in, w_out]
    if mixer is not None:
        a, mix_gate, w_o = mixer
        in_specs += [pl.BlockSpec((gg, tt, a.shape[2]), lambda i, j: (i, j, 0)), mspec, _resident(w_o.shape)]
        args += [a, mix_gate, w_o]
    if final_g is not None:
        in_specs.append(_resident((1, d)))
        args.append(final_g.reshape(1, d))
    return pl.pallas_call(
        functools.partial(_ffn_kernel, mixer=mixer is not None, final=final_g is not None),
        grid=(nseq // gg, t // tt),
        in_specs=in_specs,
        out_specs=xspec,
        out_shape=jax.ShapeDtypeStruct(x.shape, F32),
        compiler_params=_params("parallel", "parallel"),
        name="ffn_final" if final_g is not None else "ffn",
    )(*args)


def _proj_kernel(x_ref, sh_ref, sc_ref, g_ref, *refs):
    n_out = len(refs) // 2
    x = x_ref[...]
    gg, tt, d = x.shape
    h = _modnorm(x, g_ref[...], sh_ref[...], sc_ref[...]).astype(BF16).reshape(gg * tt, d)
    for w_ref, o_ref in zip(refs[:n_out], refs[n_out:]):
        o_ref[...] = _dot(h, w_ref[...]).reshape(o_ref.shape).astype(o_ref.dtype)


def _norm_proj(x, mod, g, weights, name):
    nseq, t, d = x.shape
    gg, tt = _row_tiling(nseq, t)
    shift, scale, _ = mod
    xspec = pl.BlockSpec((gg, tt, d), lambda i, j: (i, j, 0))
    mspec = pl.BlockSpec((gg, 1, d), lambda i, j: (i, 0, 0))
    return pl.pallas_call(
        _proj_kernel,
        grid=(nseq // gg, t // tt),
        in_specs=[xspec, mspec, mspec, _resident((1, d))] + [_resident(w.shape) for w in weights],
        out_specs=[pl.BlockSpec((gg, tt, w.shape[1]), lambda i, j: (i, j, 0)) for w in weights],
        out_shape=[jax.ShapeDtypeStruct((nseq, t, w.shape[1]), F32) for w in weights],
        compiler_params=_params("parallel", "parallel"),
        name=name,
    )(x, shift, scale, g.reshape(1, d), *weights)


def _sb_qkv_kernel(x_ref, sh_ref, sc_ref, g_ref, wq_ref, wkt_ref, wvt_ref,
                   q_ref, kt_ref, vt_ref, ktb_ref, vtb_ref):
    x = x_ref[...]
    gg, tt, d = x.shape
    heads = kt_ref.shape[1]
    h = _modnorm(x, g_ref[...], sh_ref[...], sc_ref[...]).astype(BF16).reshape(gg * tt, d)
    q = _dot(h, wq_ref[...]) * (SB_DH ** -0.5 * LOG2_E)
    q_ref[...] = q.reshape(q_ref.shape).astype(q_ref.dtype)
    for w_ref, o_ref, b_ref in ((wkt_ref, kt_ref, ktb_ref), (wvt_ref, vt_ref, vtb_ref)):
        yt = _dot_nt(w_ref[...], h)
        for s in range(gg):
            y = yt[:, s * tt:(s + 1) * tt].reshape(heads, SB_DH, tt)
            o_ref[s] = y
            b_ref[s] = y.astype(b_ref.dtype)


def _sb_qkv(x, mod, g, w_q, w_kt, w_vt):
    nseq, t, d = x.shape
    width = w_q.shape[1]
    heads = width // SB_DH
    gg, tt = _row_tiling(nseq, t)
    shift, scale, _ = mod
    xspec = pl.BlockSpec((gg, tt, d), lambda i, j: (i, j, 0))
    mspec = pl.BlockSpec((gg, 1, d), lambda i, j: (i, 0, 0))
    tspec = pl.BlockSpec((gg, heads, SB_DH, tt), lambda i, j: (i, 0, 0, j))
    tshape = jax.ShapeDtypeStruct((nseq, heads, SB_DH, t), F32)
    return pl.pallas_call(
        _sb_qkv_kernel,
        grid=(nseq // gg, t // tt),
        in_specs=[xspec, mspec, mspec, _resident((1, d)),
                  _resident(w_q.shape), _resident(w_kt.shape), _resident(w_vt.shape)],
        out_specs=[pl.BlockSpec((gg, tt, width), lambda i, j: (i, j, 0)), tspec, tspec, tspec, tspec],
        out_shape=[jax.ShapeDtypeStruct((nseq, t, width), BF16), tshape, tshape,
                   jax.ShapeDtypeStruct(tshape.shape, BF16), jax.ShapeDtypeStruct(tshape.shape, BF16)],
        compiler_params=_params("parallel", "parallel"),
        name="sb_qkv",
    )(x, shift, scale, g.reshape(1, d), w_q, w_kt, w_vt)


def _causal_conv(x, prev8, w):
    taps = w.shape[0]
    full = jnp.concatenate([prev8, x], axis=0)
    y = x * w[taps - 1:taps]
    for j in range(taps - 1):
        y = y + pltpu.roll(full, taps - 1 - j, 0)[SUBLANES:] * w[j:j + 1]
    return y


def _gdn_patterns(rows, heads):
    width = heads * DN_DK
    iota = lambda shape, dim: lax.broadcasted_iota(jnp.int32, shape, dim)
    head_ones = iota((width, width), 0) // DN_DK == iota((width, width), 1) // DN_DK
    rr, cc = iota((rows, rows), 0), iota((rows, rows), 1)
    chunk_ones = rr // CHUNK == cc // CHUNK
    chunk_tril = jnp.logical_and(chunk_ones, cc <= rr)
    li, lh = iota((LANES, width), 0), iota((LANES, width), 1) // DN_DK
    return [m.astype(BF16) for m in (head_ones, chunk_ones, chunk_tril, li == lh, li == lh + heads)]


def _gdn_kernel(qkv_ref, z_ref, ab_ref, s_ref, s0_ref, conv0_ref, sc0_ref,
                wconv_ref, alog_ref, dtb_ref, dng_ref, wsc_ref,
                head_ones_ref, chunk_ones_ref, chunk_tril_ref, expand_g_ref, expand_b_ref,
                y_ref, sout_ref, convout_ref, scout_ref,
                state_ref, convprev_ref, scprev_ref, o_scr):
    step = pl.program_id(1)
    gg, tt, _ = qkv_ref.shape
    heads = state_ref.shape[1]
    width = heads * DN_DK
    scw = wsc_ref.shape[1]
    nc = tt // CHUNK
    nb = gg * nc
    rows = gg * tt

    @pl.when(step == 0)
    def _():
        state_ref[...] = s0_ref[...]
        convprev_ref[...] = conv0_ref[...]
        scprev_ref[...] = sc0_ref[...]

    def per_seq(fn):
        parts = [fn(s) for s in range(gg)]
        return parts[0] if gg == 1 else jnp.concatenate(parts, axis=0)

    wc = wconv_ref[...]
    wsc = wsc_ref[...]
    qkv = _silu(per_seq(lambda s: _causal_conv(qkv_ref[s], convprev_ref[s], wc)))
    cx_parts = [s_ref[s, :, scw:2 * scw] * s_ref[s, :, 2 * scw:] for s in range(gg)]
    y_sc = per_seq(lambda s: s_ref[s, :, :scw] * _causal_conv(cx_parts[s], scprev_ref[s], wsc))
    for s in range(gg):
        convprev_ref[s] = qkv_ref[s, tt - SUBLANES:, :]
        scprev_ref[s] = cx_parts[s][tt - SUBLANES:]
    q, k, v = qkv[:, :width], qkv[:, width:2 * width], qkv[:, 2 * width:]

    head_ones = head_ones_ref[...]
    chunk_ones = chunk_ones_ref[...]
    chunk_tril = chunk_tril_ref[...]
    expand_g = expand_g_ref[...]
    expand_b = expand_b_ref[...]
    fr = lax.broadcasted_iota(jnp.int32, (rows, width), 0) % CHUNK
    ss = lax.broadcasted_iota(jnp.int32, (rows, width), 1) % DN_DK
    incl_all = ss <= fr
    eye_all = ss == fr
    t2 = lax.broadcasted_iota(jnp.int32, (CHUNK, CHUNK), 0)
    s2 = lax.broadcasted_iota(jnp.int32, (CHUNK, CHUNK), 1)
    strict = s2 < t2

    q = q * lax.rsqrt(_dot_x2(q * q, head_ones) + L2_EPS) * (DN_DK ** -0.5)
    k = k * lax.rsqrt(_dot_x2(k * k, head_ones) + L2_EPS)

    ab = per_seq(lambda s: ab_ref[s])
    g_log = -jnp.exp(alog_ref[...]) * _softplus(ab + dtb_ref[...])
    g_cum = _mdot_x3(chunk_tril, g_log)
    gexp = _dot_x3(g_cum, expand_g)
    beta = _dot_x2(_sigmoid(ab), expand_b)
    grow = _mdot_x3(chunk_ones, jnp.where(eye_all, gexp, 0.0))
    decay = jnp.where(incl_all, jnp.exp(jnp.where(incl_all, gexp - grow, 0.0)), 0.0)
    eg = jnp.exp(gexp)
    g3 = gexp.reshape(nb, CHUNK, width)
    g_end = jnp.broadcast_to(g3[:, CHUNK - 1:CHUNK, :], g3.shape).reshape(rows, width)
    k_dec = (k * jnp.exp(g_end - gexp)).astype(BF16)
    eg_end = jnp.exp(g_end)
    bv = beta * v
    bk = beta * eg * k
    qg = eg * q
    q_bf = q.astype(BF16)
    k_bf = k.astype(BF16)

    def blk(a, n, h):
        return a[n * CHUNK:(n + 1) * CHUNK, h * DN_DK:(h + 1) * DN_DK]

    chains = [(n, h) for n in range(nb) for h in range(heads)]
    kk = {c: _dot_nt(blk(k_bf, *c), blk(k_bf, *c)) for c in chains}
    p = {c: -jnp.where(strict, blk(beta, *c) * kk[c] * blk(decay, *c), 0.0) for c in chains}
    x = {c: jnp.concatenate([blk(bv, *c), blk(bk, *c)], axis=1) for c in chains}
    levels = CHUNK.bit_length() - 1
    for lvl in range(levels):
        last = lvl == levels - 1
        res = {}
        for c in chains:
            rhs = x[c] if last else jnp.concatenate([x[c], p[c]], axis=1)
            res[c] = (_dot_f32(p[c], rhs) if lvl < PRECISE_LEVELS
                      else _dot(p[c].astype(BF16), rhs.astype(BF16)))
        for c in chains:
            x[c] = x[c] + res[c][:, :2 * DN_DK]
            if not last:
                p[c] = res[c][:, 2 * DN_DK:]
    x_bf = {c: x[c].astype(BF16) for c in chains}
    cn = {c: _dot_tn(blk(k_dec, *c), x_bf[c]) for c in chains}
    qk = {c: (_dot_nt(blk(q_bf, *c), blk(k_bf, *c)) * blk(decay, *c)).astype(BF16) for c in chains}
    rp = {c: _dot(qk[c], x_bf[c]) for c in chains}
    pc = {c: jnp.concatenate([blk(qg, *c) - rp[c][:, DN_DK:], cn[c][:, DN_DK:]], axis=0).astype(BF16)
          for c in chains}
    for s in range(gg):
        state = [state_ref[s, h] for h in range(heads)]
        for i in range(nc):
            n = s * nc + i
            ps = [_dot(pc[(n, h)], state[h].astype(BF16)) for h in range(heads)]
            for h in range(heads):
                o_scr[n * CHUNK:(n + 1) * CHUNK, h * DN_DK:(h + 1) * DN_DK] = ps[h][:CHUNK] + rp[(n, h)][:, :DN_DK]
                state[h] = blk(eg_end, n, h) * state[h] - ps[h][CHUNK:] + cn[(n, h)][:, :DN_DK]
        for h in range(heads):
            state_ref[s, h] = state[h]

    o_all = o_scr[...]
    ms = _dot_x2(o_all * o_all, head_ones) * (1.0 / DN_DK)
    z = per_seq(lambda s: z_ref[s])
    o_all = o_all * lax.rsqrt(ms + NORM_EPS) * dng_ref[...] * _silu(z)
    y_ref[:, :, :width] = o_all.reshape(gg, tt, width).astype(y_ref.dtype)
    y_ref[:, :, width:] = y_sc.reshape(gg, tt, scw).astype(y_ref.dtype)

    @pl.when(step == pl.num_programs(1) - 1)
    def _():
        sout_ref[...] = state_ref[...]
        convout_ref[...] = convprev_ref[...]
        scout_ref[...] = scprev_ref[...]


def _gdn(qkv, z, ab, s, s0, conv0, sc0, wconv, alog, dtb, dng, wsc):
    nseq, t, w3 = qkv.shape
    heads = s0.shape[1]
    width = heads * DN_DK
    scw = wsc.shape[1]
    gg, tt = _row_tiling(nseq, t)
    assert tt % CHUNK == 0
    row = lambda n: pl.BlockSpec((gg, tt, n), lambda i, j: (i, j, 0))
    per_seq = lambda shp: pl.BlockSpec((gg,) + shp, lambda i, j: (i,) + (0,) * len(shp))
    patterns = _gdn_patterns(gg * tt, heads)
    return pl.pallas_call(
        _gdn_kernel,
        grid=(nseq // gg, t // tt),
        in_specs=[row(w3), row(width), row(LANES), row(3 * scw),
                  per_seq((heads, DN_DK, DN_DK)), per_seq((SUBLANES, w3)), per_seq((SUBLANES, scw)),
                  _resident(wconv.shape), _resident((1, LANES)), _resident((1, LANES)),
                  _resident((1, width)), _resident(wsc.shape)] + [_resident(m.shape) for m in patterns],
        out_specs=[row(width + scw), per_seq((heads, DN_DK, DN_DK)),
                   per_seq((SUBLANES, w3)), per_seq((SUBLANES, scw))],
        out_shape=[jax.ShapeDtypeStruct((nseq, t, width + scw), BF16),
                   jax.ShapeDtypeStruct((nseq, heads, DN_DK, DN_DK), F32),
                   jax.ShapeDtypeStruct((nseq, SUBLANES, w3), F32),
                   jax.ShapeDtypeStruct((nseq, SUBLANES, scw), F32)],
        scratch_shapes=[pltpu.VMEM((gg, heads, DN_DK, DN_DK), F32),
                        pltpu.VMEM((gg, SUBLANES, w3), F32),
                        pltpu.VMEM((gg, SUBLANES, scw), F32),
                        pltpu.VMEM((gg * tt, width), F32)],
        compiler_params=_params("parallel", "arbitrary"),
        name="gdn_sconv",
    )(qkv, z, ab, s, s0, conv0, sc0, wconv, alog, dtb, dng, wsc, *patterns)


def _sb_blocks(items, upper):
    def run(carries, scale_out):
        zs = [_dot(q, kt) for _, q, kt, _, _ in items]
        log_sig, log_rest = [], []
        for z, (_, _, _, _, mask) in zip(zs, items):
            soft = jnp.log2(1.0 + jnp.exp2(-jnp.abs(z)))
            ls = jnp.minimum(z, 0.0) - soft
            lr = ls - z
            if mask is not None:
                lr = jnp.where(mask, lr, 0.0)
            log_sig.append(ls)
            log_rest.append(lr)
        laters = [_dot(lr.astype(BF16), upper[:lr.shape[1], :lr.shape[1]]) for lr in log_rest]
        out = {}
        ws, scales = [], []
        for (chain, _, _, _, mask), ls, lr, later in zip(items, log_sig, log_rest, laters):
            carry = out[chain][1] if chain in out else carries[chain]
            w = jnp.exp2(ls + later if scale_out else ls + later + carry)
            if mask is not None:
                w = jnp.where(mask, w, 0.0)
            ws.append(w.astype(BF16))
            scales.append(jnp.exp2(carry) if scale_out else None)
            out[chain] = (out[chain][0] if chain in out else None, carry + (later[:, 0:1] + lr[:, 0:1]))
        for (chain, _, _, vt, _), w, scale in zip(items, ws, scales):
            o = _dot_nt(w, vt)
            if scale is not None:
                o = o * scale
            acc, carry = out[chain]
            out[chain] = (o if acc is None else acc + o, carry)
        return out
    return run


def _upper(n):
    j = lax.broadcasted_iota(jnp.int32, (n, n), 0)
    s = lax.broadcasted_iota(jnp.int32, (n, n), 1)
    return (j > s).astype(BF16)


def _any_alive(carries):
    top = functools.reduce(jnp.maximum, [jnp.max(c) for c in carries])
    return (top > SB_DEAD_LOG2).astype(jnp.int32)


def _sb_prompt_kernel(q_ref, kt_ref, vt_ref, o_ref):
    hp = kt_ref.shape[1]
    t = q_ref.shape[1]
    blk = min(SB_BLOCK, t)
    upper = _upper(blk)
    tq = lax.broadcasted_iota(jnp.int32, (blk, blk), 0)
    ts = lax.broadcasted_iota(jnp.int32, (blk, blk), 1)
    causal = ts < tq
    zero = jnp.zeros((blk, 1), F32)

    def q_block(qb, _):
        q0 = pl.multiple_of(qb * blk, blk)
        qs = [q_ref[0, pl.ds(q0, blk), hh * SB_DH:(hh + 1) * SB_DH] for hh in range(hp)]

        def items(k0, mask):
            return [(hh, qs[hh], kt_ref[0, hh, :, pl.ds(k0, blk)], vt_ref[0, hh, :, pl.ds(k0, blk)], mask)
                    for hh in range(hp)]

        first = _sb_blocks(items(q0, causal), upper)({hh: zero for hh in range(hp)}, True)

        def k_block(st):
            i, _, oc = st
            k0 = pl.multiple_of((qb - 1 - i) * blk, blk)
            res = _sb_blocks(items(k0, None), upper)({hh: oc[hh][1] for hh in range(hp)}, True)
            new = tuple((oc[hh][0] + res[hh][0], res[hh][1]) for hh in range(hp))
            return i + 1, _any_alive([c for _, c in new]), new

        start = tuple(first[hh] for hh in range(hp))
        done = lax.while_loop(lambda st: jnp.logical_and(st[0] < qb, st[1] > 0), k_block,
                              (jnp.int32(0), _any_alive([c for _, c in start]), start))[2]
        o = jnp.concatenate([done[hh][0] for hh in range(hp)], axis=1)
        o_ref[0, pl.ds(q0, blk), :] = o.astype(o_ref.dtype)
        return 0

    lax.fori_loop(0, t // blk, q_block, 0)


def _sb_prompt(q, kt, vt):
    b, heads, dh, t = kt.shape
    hp = min(SB_CHAINS, heads)
    assert t % min(SB_BLOCK, t) == 0 and heads % hp == 0 and (hp * dh) % LANES == 0
    qspec = pl.BlockSpec((1, t, hp * dh), lambda i, j: (i, 0, j))
    tspec = pl.BlockSpec((1, hp, dh, t), lambda i, j: (i, j, 0, 0))
    return pl.pallas_call(
        _sb_prompt_kernel,
        grid=(b, heads // hp),
        in_specs=[qspec, tspec, tspec],
        out_specs=qspec,
        out_shape=jax.ShapeDtypeStruct((b, t, heads * dh), BF16),
        compiler_params=_params("parallel", "parallel"),
        name="sb_prompt",
    )(q, kt, vt)


def _sb_sample_kernel(q_ref, kt_ref, vt_ref, ckt_hbm, cvt_hbm, o_ref, kbuf, vbuf, sem):
    hs = kt_ref.shape[1]
    t = q_ref.shape[1]
    past = ckt_hbm.shape[3]
    blk = min(SB_BLOCK, past)
    rows = hs * t
    upper = _upper(max(blk, t))
    nblk = past // blk
    group = math.gcd(nblk, SB_CACHE_GROUP)
    ngroups = nblk // group
    b = pl.program_id(0)
    h0 = pl.program_id(1) * hs

    def cache_copies(g, slot):
        k0 = pl.multiple_of(past - (g + 1) * group * blk, blk)
        window = (b, pl.ds(h0, hs), slice(None), pl.ds(k0, group * blk))
        return (pltpu.make_async_copy(ckt_hbm.at[window], kbuf.at[slot], sem.at[0, slot]),
                pltpu.make_async_copy(cvt_hbm.at[window], vbuf.at[slot], sem.at[1, slot]))

    def start_group(g, slot):
        for cp in cache_copies(g, slot):
            cp.start()

    def wait_group(g, slot):
        for cp in cache_copies(g, slot):
            cp.wait()

    start_group(0, 0)
    rh = lax.broadcasted_iota(jnp.int32, (rows, hs * SB_DH), 0) // t
    ch = lax.broadcasted_iota(jnp.int32, (rows, hs * SB_DH), 1) // SB_DH
    own = rh == ch
    q4 = q_ref[0]
    q_bd = jnp.where(own, jnp.concatenate([q4] * hs, axis=0), jnp.zeros((), q4.dtype))
    fq = lax.broadcasted_iota(jnp.int32, (rows, t), 0) % t
    fs = lax.broadcasted_iota(jnp.int32, (rows, t), 1)
    causal = fs < fq

    new = (0, q_bd, kt_ref[0].reshape(hs * SB_DH, t), vt_ref[0].reshape(hs * SB_DH, t), causal)
    first = _sb_blocks([new], upper)({0: jnp.zeros((rows, 1), F32)}, False)[0]

    def cache_item(slot, r):
        lanes = slice(r * blk, (r + 1) * blk)
        return (0, q_bd, kbuf[slot, :, :, lanes].reshape(hs * SB_DH, blk).astype(BF16),
                vbuf[slot, :, :, lanes].reshape(hs * SB_DH, blk).astype(BF16), None)

    def k_blocks(st):
        i, _, oc = st
        slot = i % 2
        wait_group(i, slot)

        @pl.when(i + 1 < ngroups)
        def _():
            start_group(i + 1, 1 - slot)

        its = [cache_item(slot, group - 1 - r) for r in range(group)]
        res = _sb_blocks(its, upper)({0: oc[1]}, False)[0]
        return i + 1, _any_alive([res[1]]), (oc[0] + res[0], res[1])

    trips, _, (o_full, _) = lax.while_loop(
        lambda st: jnp.logical_and(st[0] < ngroups, st[1] > 0), k_blocks,
        (jnp.int32(0), _any_alive([first[1]]), first))

    @pl.when(trips < ngroups)
    def _():
        wait_group(trips, trips % 2)

    o_own = jnp.where(own, o_full, 0.0)
    o = o_own[:t]
    for i in range(1, hs):
        o = o + o_own[i * t:(i + 1) * t]
    o_ref[0] = o.astype(o_ref.dtype)


def _sb_sample(q, kt, vt, ckt, cvt):
    b, heads, dh, t = kt.shape
    past = ckt.shape[3]
    hs = SB_BLOCK // t
    blk = min(SB_BLOCK, past)
    assert heads % hs == 0 and past % blk == 0
    group = math.gcd(past // blk, SB_CACHE_GROUP)
    qspec = pl.BlockSpec((1, t, hs * dh), lambda i, j: (i, 0, j))
    tspec = pl.BlockSpec((1, hs, dh, t), lambda i, j: (i, j, 0, 0))
    hbm = pl.BlockSpec(memory_space=pl.ANY)
    window = pltpu.VMEM((2, hs, dh, group * blk), ckt.dtype)
    return pl.pallas_call(
        _sb_sample_kernel,
        grid=(b, heads // hs),
        in_specs=[qspec, tspec, tspec, hbm, hbm],
        out_specs=qspec,
        out_shape=jax.ShapeDtypeStruct((b, t, heads * dh), BF16),
        scratch_shapes=[window, window, pltpu.SemaphoreType.DMA((2, 2))],
        compiler_params=_params("parallel", "parallel"),
        name="sb_sample",
    )(q, kt, vt, ckt, cvt)


def _pad_rows(state, rows):
    return jnp.pad(state, ((0, 0), (rows - state.shape[1], 0), (0, 0)))


def _trunk(x, mods, s_delta, s_qkv, s_sc, cache_t, w):
    x = _ffn(x, mods[0][0], w["norm_g"][0, 0], w["ff_in"][0, 0], w["ff_out"][0, 0])
    qkv, z, ab, s = _norm_proj(x, mods[0][1], w["norm_g"][0, 1],
                               [w["ab_qkv"], w["ab_z"], w["ab_ab"], w["ab_s"]], "ab_in_proj")
    y, new_delta, conv8, sc8 = _gdn(qkv, z, ab, s, s_delta, _pad_rows(s_qkv, SUBLANES),
                                    _pad_rows(s_sc, SUBLANES), w["ab_conv"], w["alog"], w["dtb"],
                                    w["dng"], w["sc_conv"])
    x = _ffn(x, mods[0][2], w["norm_g"][0, 2], w["ff_in"][0, 1], w["ff_out"][0, 1],
             mixer=(y, mods[0][1][2], w["ab_out"]))
    x = _ffn(x, mods[1][0], w["norm_g"][1, 0], w["ff_in"][1, 0], w["ff_out"][1, 0])
    q, kt, vt, ktb, vtb = _sb_qkv(x, mods[1][1], w["norm_g"][1, 1], w["sb_q"], w["sb_kt"], w["sb_vt"])
    o = _sb_prompt(q, ktb, vtb) if cache_t is None else _sb_sample(q, ktb, vtb, cache_t[0], cache_t[1])
    y_out = _ffn(x, mods[1][2], w["norm_g"][1, 2], w["ff_in"][1, 1], w["ff_out"][1, 1],
                 mixer=(o, mods[1][1][2], w["sb_out"]), final_g=w["final_g"])
    new_qkv = conv8[:, SUBLANES - (DN_CONV - 1):]
    new_sc = sc8[:, SUBLANES - (SC_CONV - 1):]
    new_k = jnp.swapaxes(kt, -1, -2)
    new_v = jnp.swapaxes(vt, -1, -2)
    return y_out, new_delta[None], new_qkv[None], new_sc[None], new_k[None], new_v[None]


def kernel(x_prompt, x_sample, c_prompt, c_sample, state_delta, state_qkv_conv, state_sconv, cache_k, cache_v,
           norm_g, ada_w, ada_b, ff_w_in, ff_w_out, ab_w_in, ab_conv_qkv, dn_A_log, dn_dt_bias, dn_norm_g,
           sc_conv, ab_w_out, sb_w_qkv, sb_w_out, final_g):
    depth, _, d = norm_g.shape
    assert depth == 2 and ab_w_in.shape[0] == 1 and sb_w_qkv.shape[0] == 1
    bp, bs = x_prompt.shape[0], x_sample.shape[0]
    heads = dn_A_log.shape[1]
    width = heads * DN_DK
    scw = sc_conv.shape[2]
    sbw = sb_w_qkv.shape[2] // 3
    assert 2 * heads <= LANES

    o_z, o_a, o_s = 3 * width, 4 * width, 4 * width + 2 * heads
    w_in = ab_w_in[0]
    w_sb = sb_w_qkv[0].astype(BF16)
    pad_lane = lambda a: jnp.pad(a, ((0, 0), (0, LANES - a.shape[1])))
    w = {
        "norm_g": norm_g, "final_g": final_g,
        "ff_in": ff_w_in.astype(BF16), "ff_out": ff_w_out.astype(BF16),
        "ab_qkv": w_in[:, :o_z].astype(BF16), "ab_z": w_in[:, o_z:o_a].astype(BF16),
        "ab_ab": pad_lane(w_in[:, o_a:o_s]).astype(BF16), "ab_s": w_in[:, o_s:].astype(BF16),
        "ab_conv": ab_conv_qkv[0], "sc_conv": sc_conv[0],
        "alog": pad_lane(dn_A_log), "dtb": pad_lane(dn_dt_bias),
        "dng": jnp.tile(dn_norm_g[0], heads).reshape(1, width),
        "ab_out": ab_w_out[0].astype(BF16),
        "sb_q": w_sb[:, :sbw], "sb_kt": w_sb[:, sbw:2 * sbw].T, "sb_vt": w_sb[:, 2 * sbw:].T,
        "sb_out": sb_w_out[0].astype(BF16),
    }

    c_all = jnp.concatenate([c_prompt, c_sample], axis=0)
    rows = -(-(bp + bs) // BF16_ROWS) * BF16_ROWS
    c_all = jnp.pad(c_all, ((0, rows - (bp + bs)), (0, 0)))
    mod = _ada_modulation(c_all, ada_w.astype(BF16), ada_b)[:, :bp + bs]
    mod = mod.reshape(depth, bp + bs, N_SUB, 3, 1, d)

    def mods_for(lo, hi):
        return [[tuple(mod[l, lo:hi, s, i] for i in range(3)) for s in range(N_SUB)] for l in range(depth)]

    zeros = lambda shape: jnp.zeros(shape, x_prompt.dtype)
    out_p = _trunk(x_prompt, mods_for(0, bp), zeros((bp, heads, DN_DK, DN_DK)),
                   zeros((bp, DN_CONV - 1, 3 * width)), zeros((bp, SC_CONV - 1, scw)), None, w)
    cache_t = (jnp.swapaxes(cache_k[0], -1, -2), jnp.swapaxes(cache_v[0], -1, -2))
    out_s = _trunk(x_sample, mods_for(bp, bp + bs), state_delta[0], state_qkv_conv[0], state_sconv[0], cache_t, w)
    return (out_p[0], out_s[0]) + out_p[1:] + out_s[1:]
```

```python
import functools
import math

import jax
import jax.numpy as jnp
from jax import lax
from jax.experimental import pallas as pl
from jax.experimental.pallas import tpu as pltpu

F32 = jnp.float32
BF16 = jnp.bfloat16

NORM_EPS = 1e-6
L2_EPS = 1e-6
LOG2_E = math.log2(math.e)
N_SUB = 3
CHUNK = 64
DN_DK = 64
DN_CONV = 4
PRECISE_LEVELS = 5
SC_CONV = 3
SB_DH = 64
ROW_TILE = 512
FF_TILE = 256
SB_BLOCK = 256
SB_CHAINS = 4
SB_CACHE_GROUP = 2
SB_DEAD_LOG2 = -160.0
SUBLANES = 8
BF16_ROWS = 16
LANES = 128
VMEM_LIMIT = 56 * 1024 * 1024


def _params(*semantics):
    return pltpu.CompilerParams(dimension_semantics=semantics, vmem_limit_bytes=VMEM_LIMIT)


def _resident(shape):
    zeros = (0,) * len(shape)
    return pl.BlockSpec(shape, lambda *_: zeros, pipeline_mode=pl.Buffered(1))


def _row_tiling(nseq, t):
    if t >= ROW_TILE:
        assert t % ROW_TILE == 0
        return 1, ROW_TILE
    g = min(nseq, ROW_TILE // t)
    assert nseq % g == 0
    return g, t


def _dot(a, b):
    return jnp.dot(a, b, preferred_element_type=F32)


def _dot_nt(a, b):
    return lax.dot_general(a, b, (((1,), (1,)), ((), ())), preferred_element_type=F32)


def _dot_tn(a, b):
    return lax.dot_general(a, b, (((0,), (0,)), ((), ())), preferred_element_type=F32)


def _split(x):
    hi = x.astype(BF16)
    lo = (x - hi.astype(F32)).astype(BF16)
    return hi, lo


def _dot_x2(x, m_bf16):
    hi, lo = _split(x)
    return _dot(hi, m_bf16) + _dot(lo, m_bf16)


def _split3(x):
    hi = x.astype(BF16)
    r = x - hi.astype(F32)
    mid = r.astype(BF16)
    lo = (r - mid.astype(F32)).astype(BF16)
    return hi, mid, lo


def _dot_x3(x, m_bf16):
    hi, mid, lo = _split3(x)
    return _dot(hi, m_bf16) + (_dot(mid, m_bf16) + _dot(lo, m_bf16))


def _mdot_x3(m_bf16, x):
    hi, mid, lo = _split3(x)
    return _dot(m_bf16, hi) + (_dot(m_bf16, mid) + _dot(m_bf16, lo))


def _dot_f32(x, y):
    xh = x.astype(BF16).astype(F32)
    yh, yl = _split(y)
    lhs = jnp.concatenate([xh, x - xh, xh], axis=1).astype(BF16)
    return _dot(lhs, jnp.concatenate([yh, yh, yl], axis=0))


def _sigmoid(x):
    return 1.0 / (1.0 + jnp.exp(-x))


def _silu(x):
    return x * _sigmoid(x)


def _softplus(x):
    return jnp.maximum(x, 0.0) + jnp.log(1.0 + jnp.exp(-jnp.abs(x)))


def _rmsnorm(x, g):
    ms = jnp.mean(x * x, axis=-1, keepdims=True)
    return x * lax.rsqrt(ms + NORM_EPS) * g


def _modnorm(x, g, shift, scale):
    return _rmsnorm(x, g) * (1.0 + scale) + shift


def _ada_kernel(c_ref, w_ref, b_ref, o_ref):
    cond = _silu(c_ref[...]).astype(BF16)
    o_ref[0] = _dot(cond, w_ref[0]) + b_ref[0]


def _ada_modulation(c_all, ada_w, ada_b):
    depth, d, n = ada_w.shape
    r = c_all.shape[0]
    tn = 1024 if n % 1024 == 0 else n
    return pl.pallas_call(
        _ada_kernel,
        grid=(depth, n // tn),
        in_specs=[
            pl.BlockSpec((r, d), lambda l, j: (0, 0)),
            pl.BlockSpec((1, d, tn), lambda l, j: (l, 0, j)),
            pl.BlockSpec((1, 1, tn), lambda l, j: (l, 0, j)),
        ],
        out_specs=pl.BlockSpec((1, r, tn), lambda l, j: (l, 0, j)),
        out_shape=jax.ShapeDtypeStruct((depth, r, n), F32),
        compiler_params=_params("parallel", "parallel"),
        name="ada_modulation",
    )(c_all, ada_w, ada_b.reshape(depth, 1, n))


def _ffn_kernel(x_ref, sh_ref, sc_ref, gt_ref, g_ref, win_ref, wout_ref, *rest, mixer, final):
    o_ref = rest[-1]
    x = x_ref[...]
    gg, tt, d = x.shape
    f = wout_ref.shape[0]
    if mixer:
        a_ref, mg_ref, wo_ref = rest[:3]
        rest = rest[3:]
        mix = _dot(a_ref[...].reshape(gg * tt, a_ref.shape[2]), wo_ref[...])
        x = x + mg_ref[...] * mix.reshape(gg, tt, d)
    h = _modnorm(x, g_ref[...], sh_ref[...], sc_ref[...]).astype(BF16).reshape(gg * tt, d)
    acc = jnp.zeros((gg * tt, d), F32)
    for j in range(f // FF_TILE):
        lo = j * FF_TILE
        gate = _dot(h, win_ref[:, lo:lo + FF_TILE])
        up = _dot(h, win_ref[:, f + lo:f + lo + FF_TILE])
        act = (_silu(gate) * up).astype(BF16)
        acc = acc + _dot(act, wout_ref[lo:lo + FF_TILE, :])
    y = x + 0.5 * gt_ref[...] * acc.reshape(gg, tt, d)
    if final:
        y = _rmsnorm(y, rest[0][...])
    o_ref[...] = y


def _ffn(x, mod, g, w_in, w_out, mixer=None, final_g=None):
    nseq, t, d = x.shape
    f = w_out.shape[0]
    assert f % FF_TILE == 0
    gg, tt = _row_tiling(nseq, t)
    shift, scale, gate = mod
    xspec = pl.BlockSpec((gg, tt, d), lambda i, j: (i, j, 0))
    mspec = pl.BlockSpec((gg, 1, d), lambda i, j: (i, 0, 0))
    in_specs = [xspec, mspec, mspec, mspec, _resident((1, d)), _resident((d, 2 * f)), _resident((f, d))]
    args = [x, shift, scale, gate, g.reshape(1, d), w_in, w_out]
    if mixer is not None:
        a, mix_gate, w_o = mixer
        in_specs += [pl.BlockSpec((gg, tt, a.shape[2]), lambda i, j: (i, j, 0)), mspec, _resident(w_o.shape)]
        args += [a, mix_gate, w_o]
    if final_g is not None:
        in_specs.append(_resident((1, d)))
        args.append(final_g.reshape(1, d))
    return pl.pallas_call(
        functools.partial(_ffn_kernel, mixer=mixer is not None, final=final_g is not None),
        grid=(nseq // gg, t // tt),
        in_specs=in_specs,
        out_specs=xspec,
        out_shape=jax.ShapeDtypeStruct(x.shape, F32),
        compiler_params=_params("parallel", "parallel"),
        name="ffn_final" if final_g is not None else "ffn",
    )(*args)


def _proj_kernel(x_ref, sh_ref, sc_ref, g_ref, *refs):
    n_out = len(refs) // 2
    x = x_ref[...]
    gg, tt, d = x.shape
    h = _modnorm(x, g_ref[...], sh_ref[...], sc_ref[...]).astype(BF16).reshape(gg * tt, d)
    for w_ref, o_ref in zip(refs[:n_out], refs[n_out:]):
        o_ref[...] = _dot(h, w_ref[...]).reshape(o_ref.shape).astype(o_ref.dtype)


def _norm_proj(x, mod, g, weights, name):
    nseq, t, d = x.shape
    gg, tt = _row_tiling(nseq, t)
    shift, scale, _ = mod
    xspec = pl.BlockSpec((gg, tt, d), lambda i, j: (i, j, 0))
    mspec = pl.BlockSpec((gg, 1, d), lambda i, j: (i, 0, 0))
    return pl.pallas_call(
        _proj_kernel,
        grid=(nseq // gg, t // tt),
        in_specs=[xspec, mspec, mspec, _resident((1, d))] + [_resident(w.shape) for w in weights],
        out_specs=[pl.BlockSpec((gg, tt, w.shape[1]), lambda i, j: (i, j, 0)) for w in weights],
        out_shape=[jax.ShapeDtypeStruct((nseq, t, w.shape[1]), F32) for w in weights],
        compiler_params=_params("parallel", "parallel"),
        name=name,
    )(x, shift, scale, g.reshape(1, d), *weights)


def _sb_qkv_kernel(x_ref, sh_ref, sc_ref, g_ref, wq_ref, wkt_ref, wvt_ref,
                   q_ref, kt_ref, vt_ref, ktb_ref, vtb_ref):
    x = x_ref[...]
    gg, tt, d = x.shape
    heads = kt_ref.shape[1]
    h = _modnorm(x, g_ref[...], sh_ref[...], sc_ref[...]).astype(BF16).reshape(gg * tt, d)
    q = _dot(h, wq_ref[...]) * (SB_DH ** -0.5 * LOG2_E)
    q_ref[...] = q.reshape(q_ref.shape).astype(q_ref.dtype)
    for w_ref, o_ref, b_ref in ((wkt_ref, kt_ref, ktb_ref), (wvt_ref, vt_ref, vtb_ref)):
        yt = _dot_nt(w_ref[...], h)
        for s in range(gg):
            y = yt[:, s * tt:(s + 1) * tt].reshape(heads, SB_DH, tt)
            o_ref[s] = y
            b_ref[s] = y.astype(b_ref.dtype)


def _sb_qkv(x, mod, g, w_q, w_kt, w_vt):
    nseq, t, d = x.shape
    width = w_q.shape[1]
    heads = width // SB_DH
    gg, tt = _row_tiling(nseq, t)
    shift, scale, _ = mod
    xspec = pl.BlockSpec((gg, tt, d), lambda i, j: (i, j, 0))
    mspec = pl.BlockSpec((gg, 1, d), lambda i, j: (i, 0, 0))
    tspec = pl.BlockSpec((gg, heads, SB_DH, tt), lambda i, j: (i, 0, 0, j))
    tshape = jax.ShapeDtypeStruct((nseq, heads, SB_DH, t), F32)
    return pl.pallas_call(
        _sb_qkv_kernel,
        grid=(nseq // gg, t // tt),
        in_specs=[xspec, mspec, mspec, _resident((1, d)),
                  _resident(w_q.shape), _resident(w_kt.shape), _resident(w_vt.shape)],
        out_specs=[pl.BlockSpec((gg, tt, width), lambda i, j: (i, j, 0)), tspec, tspec, tspec, tspec],
        out_shape=[jax.ShapeDtypeStruct((nseq, t, width), BF16), tshape, tshape,
                   jax.ShapeDtypeStruct(tshape.shape, BF16), jax.ShapeDtypeStruct(tshape.shape, BF16)],
        compiler_params=_params("parallel", "parallel"),
        name="sb_qkv",
    )(x, shift, scale, g.reshape(1, d), w_q, w_kt, w_vt)


def _causal_conv(x, prev8, w):
    taps = w.shape[0]
    full = jnp.concatenate([prev8, x], axis=0)
    y = x * w[taps - 1:taps]
    for j in range(taps - 1):
        y = y + pltpu.roll(full, taps - 1 - j, 0)[SUBLANES:] * w[j:j + 1]
    return y


def _gdn_patterns(rows, heads):
    width = heads * DN_DK
    iota = lambda shape, dim: lax.broadcasted_iota(jnp.int32, shape, dim)
    head_ones = iota((width, width), 0) // DN_DK == iota((width, width), 1) // DN_DK
    rr, cc = iota((rows, rows), 0), iota((rows, rows), 1)
    chunk_ones = rr // CHUNK == cc // CHUNK
    chunk_tril = jnp.logical_and(chunk_ones, cc <= rr)
    li, lh = iota((LANES, width), 0), iota((LANES, width), 1) // DN_DK
    return [m.astype(BF16) for m in (head_ones, chunk_ones, chunk_tril, li == lh, li == lh + heads)]


def _gdn_kernel(qkv_ref, z_ref, ab_ref, s_ref, s0_ref, conv0_ref, sc0_ref,
                wconv_ref, alog_ref, dtb_ref, dng_ref, wsc_ref,
                head_ones_ref, chunk_ones_ref, chunk_tril_ref, expand_g_ref, expand_b_ref,
                y_ref, sout_ref, convout_ref, scout_ref,
                state_ref, convprev_ref, scprev_ref, o_scr):
    step = pl.program_id(1)
    gg, tt, _ = qkv_ref.shape
    heads = state_ref.shape[1]
    width = heads * DN_DK
    scw = wsc_ref.shape[1]
    nc = tt // CHUNK
    nb = gg * nc
    rows = gg * tt

    @pl.when(step == 0)
    def _():
        state_ref[...] = s0_ref[...]
        convprev_ref[...] = conv0_ref[...]
        scprev_ref[...] = sc0_ref[...]

    def per_seq(fn):
        parts = [fn(s) for s in range(gg)]
        return parts[0] if gg == 1 else jnp.concatenate(parts, axis=0)

    wc = wconv_ref[...]
    wsc = wsc_ref[...]
    qkv = _silu(per_seq(lambda s: _causal_conv(qkv_ref[s], convprev_ref[s], wc)))
    cx_parts = [s_ref[s, :, scw:2 * scw] * s_ref[s, :, 2 * scw:] for s in range(gg)]
    y_sc = per_seq(lambda s: s_ref[s, :, :scw] * _causal_conv(cx_parts[s], scprev_ref[s], wsc))
    for s in range(gg):
        convprev_ref[s] = qkv_ref[s, tt - SUBLANES:, :]
        scprev_ref[s] = cx_parts[s][tt - SUBLANES:]
    q, k, v = qkv[:, :width], qkv[:, width:2 * width], qkv[:, 2 * width:]

    head_ones = head_ones_ref[...]
    chunk_ones = chunk_ones_ref[...]
    chunk_tril = chunk_tril_ref[...]
    expand_g = expand_g_ref[...]
    expand_b = expand_b_ref[...]
    fr = lax.broadcasted_iota(jnp.int32, (rows, width), 0) % CHUNK
    ss = lax.broadcasted_iota(jnp.int32, (rows, width), 1) % DN_DK
    incl_all = ss <= fr
    eye_all = ss == fr
    t2 = lax.broadcasted_iota(jnp.int32, (CHUNK, CHUNK), 0)
    s2 = lax.broadcasted_iota(jnp.int32, (CHUNK, CHUNK), 1)
    strict = s2 < t2

    q = q * lax.rsqrt(_dot_x2(q * q, head_ones) + L2_EPS) * (DN_DK ** -0.5)
    k = k * lax.rsqrt(_dot_x2(k * k, head_ones) + L2_EPS)

    ab = per_seq(lambda s: ab_ref[s])
    g_log = -jnp.exp(alog_ref[...]) * _softplus(ab + dtb_ref[...])
    g_cum = _mdot_x3(chunk_tril, g_log)
    gexp = _dot_x3(g_cum, expand_g)
    beta = _dot_x2(_sigmoid(ab), expand_b)
    grow = _mdot_x3(chunk_ones, jnp.where(eye_all, gexp, 0.0))
    decay = jnp.where(incl_all, jnp.exp(jnp.where(incl_all, gexp - grow, 0.0)), 0.0)
    eg = jnp.exp(gexp)
    g3 = gexp.reshape(nb, CHUNK, width)
    g_end = jnp.broadcast_to(g3[:, CHUNK - 1:CHUNK, :], g3.shape).reshape(rows, width)
    k_dec = (k * jnp.exp(g_end - gexp)).astype(BF16)
    eg_end = jnp.exp(g_end)
    bv = beta * v
    bk = beta * eg * k
    qg = eg * q
    q_bf = q.astype(BF16)
    k_bf = k.astype(BF16)

    def blk(a, n, h):
        return a[n * CHUNK:(n + 1) * CHUNK, h * DN_DK:(h + 1) * DN_DK]

    chains = [(n, h) for n in range(nb) for h in range(heads)]
    kk = {c: _dot_nt(blk(k_bf, *c), blk(k_bf, *c)) for c in chains}
    p = {c: -jnp.where(strict, blk(beta, *c) * kk[c] * blk(decay, *c), 0.0) for c in chains}
    x = {c: jnp.concatenate([blk(bv, *c), blk(bk, *c)], axis=1) for c in chains}
    levels = CHUNK.bit_length() - 1
    for lvl in range(levels):
        last = lvl == levels - 1
        res = {}
        for c in chains:
            rhs = x[c] if last else jnp.concatenate([x[c], p[c]], axis=1)
            res[c] = (_dot_f32(p[c], rhs) if lvl < PRECISE_LEVELS
                      else _dot(p[c].astype(BF16), rhs.astype(BF16)))
        for c in chains:
            x[c] = x[c] + res[c][:, :2 * DN_DK]
            if not last:
                p[c] = res[c][:, 2 * DN_DK:]
    x_bf = {c: x[c].astype(BF16) for c in chains}
    cn = {c: _dot_tn(blk(k_dec, *c), x_bf[c]) for c in chains}
    qk = {c: (_dot_nt(blk(q_bf, *c), blk(k_bf, *c)) * blk(decay, *c)).astype(BF16) for c in chains}
    rp = {c: _dot(qk[c], x_bf[c]) for c in chains}
    pc = {c: jnp.concatenate([blk(qg, *c) - rp[c][:, DN_DK:], cn[c][:, DN_DK:]], axis=0).astype(BF16)
          for c in chains}
    for s in range(gg):
        state = [state_ref[s, h] for h in range(heads)]
        for i in range(nc):
            n = s * nc + i
            ps = [_dot(pc[(n, h)], state[h].astype(BF16)) for h in range(heads)]
            for h in range(heads):
                o_scr[n * CHUNK:(n + 1) * CHUNK, h * DN_DK:(h + 1) * DN_DK] = ps[h][:CHUNK] + rp[(n, h)][:, :DN_DK]
                state[h] = blk(eg_end, n, h) * state[h] - ps[h][CHUNK:] + cn[(n, h)][:, :DN_DK]
        for h in range(heads):
            state_ref[s, h] = state[h]

    o_all = o_scr[...]
    ms = _dot_x2(o_all * o_all, head_ones) * (1.0 / DN_DK)
    z = per_seq(lambda s: z_ref[s])
    o_all = o_all * lax.rsqrt(ms + NORM_EPS) * dng_ref[...] * _silu(z)
    y_ref[:, :, :width] = o_all.reshape(gg, tt, width).astype(y_ref.dtype)
    y_ref[:, :, width:] = y_sc.reshape(gg, tt, scw).astype(y_ref.dtype)

    @pl.when(step == pl.num_programs(1) - 1)
    def _():
        sout_ref[...] = state_ref[...]
        convout_ref[...] = convprev_ref[...]
        scout_ref[...] = scprev_ref[...]


def _gdn(qkv, z, ab, s, s0, conv0, sc0, wconv, alog, dtb, dng, wsc):
    nseq, t, w3 = qkv.shape
    heads = s0.shape[1]
    width = heads * DN_DK
    scw = wsc.shape[1]
    gg, tt = _row_tiling(nseq, t)
    assert tt % CHUNK == 0
    row = lambda n: pl.BlockSpec((gg, tt, n), lambda i, j: (i, j, 0))
    per_seq = lambda shp: pl.BlockSpec((gg,) + shp, lambda i, j: (i,) + (0,) * len(shp))
    patterns = _gdn_patterns(gg * tt, heads)
    return pl.pallas_call(
        _gdn_kernel,
        grid=(nseq // gg, t // tt),
        in_specs=[row(w3), row(width), row(LANES), row(3 * scw),
                  per_seq((heads, DN_DK, DN_DK)), per_seq((SUBLANES, w3)), per_seq((SUBLANES, scw)),
                  _resident(wconv.shape), _resident((1, LANES)), _resident((1, LANES)),
                  _resident((1, width)), _resident(wsc.shape)] + [_resident(m.shape) for m in patterns],
        out_specs=[row(width + scw), per_seq((heads, DN_DK, DN_DK)),
                   per_seq((SUBLANES, w3)), per_seq((SUBLANES, scw))],
        out_shape=[jax.ShapeDtypeStruct((nseq, t, width + scw), BF16),
                   jax.ShapeDtypeStruct((nseq, heads, DN_DK, DN_DK), F32),
                   jax.ShapeDtypeStruct((nseq, SUBLANES, w3), F32),
                   jax.ShapeDtypeStruct((nseq, SUBLANES, scw), F32)],
        scratch_shapes=[pltpu.VMEM((gg, heads, DN_DK, DN_DK), F32),
                        pltpu.VMEM((gg, SUBLANES, w3), F32),
                        pltpu.VMEM((gg, SUBLANES, scw), F32),
                        pltpu.VMEM((gg * tt, width), F32)],
        compiler_params=_params("parallel", "arbitrary"),
        name="gdn_sconv",
    )(qkv, z, ab, s, s0, conv0, sc0, wconv, alog, dtb, dng, wsc, *patterns)


def _sb_blocks(items, upper):
    def run(carries, scale_out):
        zs = [_dot(q, kt) for _, q, kt, _, _ in items]
        log_sig, log_rest = [], []
        for z, (_, _, _, _, mask) in zip(zs, items):
            soft = jnp.log2(1.0 + jnp.exp2(-jnp.abs(z)))
            ls = jnp.minimum(z, 0.0) - soft
            lr = ls - z
            if mask is not None:
                lr = jnp.where(mask, lr, 0.0)
            log_sig.append(ls)
            log_rest.append(lr)
        laters = [_dot(lr.astype(BF16), upper[:lr.shape[1], :lr.shape[1]]) for lr in log_rest]
        out = {}
        ws, scales = [], []
        for (chain, _, _, _, mask), ls, lr, later in zip(items, log_sig, log_rest, laters):
            carry = out[chain][1] if chain in out else carries[chain]
            w = jnp.exp2(ls + later if scale_out else ls + later + carry)
            if mask is not None:
                w = jnp.where(mask, w, 0.0)
            ws.append(w.astype(BF16))
            scales.append(jnp.exp2(carry) if scale_out else None)
            out[chain] = (out[chain][0] if chain in out else None, carry + (later[:, 0:1] + lr[:, 0:1]))
        for (chain, _, _, vt, _), w, scale in zip(items, ws, scales):
            o = _dot_nt(w, vt)
            if scale is not None:
                o = o * scale
            acc, carry = out[chain]
            out[chain] = (o if acc is None else acc + o, carry)
        return out
    return run


def _upper(n):
    j = lax.broadcasted_iota(jnp.int32, (n, n), 0)
    s = lax.broadcasted_iota(jnp.int32, (n, n), 1)
    return (j > s).astype(BF16)


def _any_alive(carries):
    top = functools.reduce(jnp.maximum, [jnp.max(c) for c in carries])
    return (top > SB_DEAD_LOG2).astype(jnp.int32)


def _sb_prompt_kernel(q_ref, kt_ref, vt_ref, o_ref):
    hp = kt_ref.shape[1]
    t = q_ref.shape[1]
    blk = min(SB_BLOCK, t)
    upper = _upper(blk)
    tq = lax.broadcasted_iota(jnp.int32, (blk, blk), 0)
    ts = lax.broadcasted_iota(jnp.int32, (blk, blk), 1)
    causal = ts < tq
    zero = jnp.zeros((blk, 1), F32)

    def q_block(qb, _):
        q0 = pl.multiple_of(qb * blk, blk)
        qs = [q_ref[0, pl.ds(q0, blk), hh * SB_DH:(hh + 1) * SB_DH] for hh in range(hp)]

        def items(k0, mask):
            return [(hh, qs[hh], kt_ref[0, hh, :, pl.ds(k0, blk)], vt_ref[0, hh, :, pl.ds(k0, blk)], mask)
                    for hh in range(hp)]

        first = _sb_blocks(items(q0, causal), upper)({hh: zero for hh in range(hp)}, True)

        def k_block(st):
            i, _, oc = st
            k0 = pl.multiple_of((qb - 1 - i) * blk, blk)
            res = _sb_blocks(items(k0, None), upper)({hh: oc[hh][1] for hh in range(hp)}, True)
            new = tuple((oc[hh][0] + res[hh][0], res[hh][1]) for hh in range(hp))
            return i + 1, _any_alive([c for _, c in new]), new

        start = tuple(first[hh] for hh in range(hp))
        done = lax.while_loop(lambda st: jnp.logical_and(st[0] < qb, st[1] > 0), k_block,
                              (jnp.int32(0), _any_alive([c for _, c in start]), start))[2]
        o = jnp.concatenate([done[hh][0] for hh in range(hp)], axis=1)
        o_ref[0, pl.ds(q0, blk), :] = o.astype(o_ref.dtype)
        return 0

    lax.fori_loop(0, t // blk, q_block, 0)


def _sb_prompt(q, kt, vt):
    b, heads, dh, t = kt.shape
    hp = min(SB_CHAINS, heads)
    assert t % min(SB_BLOCK, t) == 0 and heads % hp == 0 and (hp * dh) % LANES == 0
    qspec = pl.BlockSpec((1, t, hp * dh), lambda i, j: (i, 0, j))
    tspec = pl.BlockSpec((1, hp, dh, t), lambda i, j: (i, j, 0, 0))
    return pl.pallas_call(
        _sb_prompt_kernel,
        grid=(b, heads // hp),
        in_specs=[qspec, tspec, tspec],
        out_specs=qspec,
        out_shape=jax.ShapeDtypeStruct((b, t, heads * dh), BF16),
        compiler_params=_params("parallel", "parallel"),
        name="sb_prompt",
    )(q, kt, vt)


def _sb_sample_kernel(q_ref, kt_ref, vt_ref, ckt_hbm, cvt_hbm, o_ref, kbuf, vbuf, sem):
    hs = kt_ref.shape[1]
    t = q_ref.shape[1]
    past = ckt_hbm.shape[3]
    blk = min(SB_BLOCK, past)
    rows = hs * t
    upper = _upper(max(blk, t))
    nblk = past // blk
    group = math.gcd(nblk, SB_CACHE_GROUP)
    ngroups = nblk // group
    nj = pl.num_programs(1)
    step = pl.program_id(0) * nj + pl.program_id(1)
    older_slot = 2

    def cache_copies(at_step, g, slot):
        k0 = pl.multiple_of(past - (g + 1) * group * blk, blk)
        window = (at_step // nj, pl.ds((at_step % nj) * hs, hs), slice(None), pl.ds(k0, group * blk))
        return (pltpu.make_async_copy(ckt_hbm.at[window], kbuf.at[slot], sem.at[0, slot]),
                pltpu.make_async_copy(cvt_hbm.at[window], vbuf.at[slot], sem.at[1, slot]))

    def start_group(at_step, g, slot):
        for cp in cache_copies(at_step, g, slot):
            cp.start()

    def wait_group(at_step, g, slot):
        for cp in cache_copies(at_step, g, slot):
            cp.wait()

    @pl.when(step == 0)
    def _():
        start_group(step, 0, 0)

    @pl.when(step + 1 < pl.num_programs(0) * nj)
    def _():
        start_group(step + 1, 0, (step + 1) % 2)

    rh = lax.broadcasted_iota(jnp.int32, (rows, hs * SB_DH), 0) // t
    ch = lax.broadcasted_iota(jnp.int32, (rows, hs * SB_DH), 1) // SB_DH
    own = rh == ch
    q4 = q_ref[0]
    q_bd = jnp.where(own, jnp.concatenate([q4] * hs, axis=0), jnp.zeros((), q4.dtype))
    fq = lax.broadcasted_iota(jnp.int32, (rows, t), 0) % t
    fs = lax.broadcasted_iota(jnp.int32, (rows, t), 1)
    causal = fs < fq

    new = (0, q_bd, kt_ref[0].reshape(hs * SB_DH, t), vt_ref[0].reshape(hs * SB_DH, t), causal)
    first = _sb_blocks([new], upper)({0: jnp.zeros((rows, 1), F32)}, False)[0]

    def cache_item(slot, r):
        lanes = slice(r * blk, (r + 1) * blk)
        return (0, q_bd, kbuf[slot, :, :, lanes].reshape(hs * SB_DH, blk).astype(BF16),
                vbuf[slot, :, :, lanes].reshape(hs * SB_DH, blk).astype(BF16), None)

    def advance(slot, oc):
        its = [cache_item(slot, group - 1 - r) for r in range(group)]
        res = _sb_blocks(its, upper)({0: oc[1]}, False)[0]
        return _any_alive([res[1]]), (oc[0] + res[0], res[1])

    wait_group(step, 0, step % 2)
    alive, acc = advance(step % 2, first)

    def older_group(st):
        g, _, oc = st
        start_group(step, g, older_slot)
        wait_group(step, g, older_slot)
        live, new = advance(older_slot, oc)
        return g + 1, live, new

    o_full, _ = lax.while_loop(lambda st: jnp.logical_and(st[0] < ngroups, st[1] > 0), older_group,
                               (jnp.int32(1), alive, acc))[2]

    o_own = jnp.where(own, o_full, 0.0)
    o = o_own[:t]
    for i in range(1, hs):
        o = o + o_own[i * t:(i + 1) * t]
    o_ref[0] = o.astype(o_ref.dtype)


def _sb_sample(q, kt, vt, ckt, cvt):
    b, heads, dh, t = kt.shape
    past = ckt.shape[3]
    hs = SB_BLOCK // t
    blk = min(SB_BLOCK, past)
    assert heads % hs == 0 and past % blk == 0
    group = math.gcd(past // blk, SB_CACHE_GROUP)
    qspec = pl.BlockSpec((1, t, hs * dh), lambda i, j: (i, 0, j))
    tspec = pl.BlockSpec((1, hs, dh, t), lambda i, j: (i, j, 0, 0))
    hbm = pl.BlockSpec(memory_space=pl.ANY)
    slots = 3
    window = pltpu.VMEM((slots, hs, dh, group * blk), ckt.dtype)
    return pl.pallas_call(
        _sb_sample_kernel,
        grid=(b, heads // hs),
        in_specs=[qspec, tspec, tspec, hbm, hbm],
        out_specs=qspec,
        out_shape=jax.ShapeDtypeStruct((b, t, heads * dh), BF16),
        scratch_shapes=[window, window, pltpu.SemaphoreType.DMA((2, slots))],
        compiler_params=_params("arbitrary", "arbitrary"),
        name="sb_sample",
    )(q, kt, vt, ckt, cvt)


def _pad_rows(state, rows):
    return jnp.pad(state, ((0, 0), (rows - state.shape[1], 0), (0, 0)))


def _trunk(x, mods, s_delta, s_qkv, s_sc, cache_t, w):
    x = _ffn(x, mods[0][0], w["norm_g"][0, 0], w["ff_in"][0, 0], w["ff_out"][0, 0])
    qkv, z, ab, s = _norm_proj(x, mods[0][1], w["norm_g"][0, 1],
                               [w["ab_qkv"], w["ab_z"], w["ab_ab"], w["ab_s"]], "ab_in_proj")
    y, new_delta, conv8, sc8 = _gdn(qkv, z, ab, s, s_delta, _pad_rows(s_qkv, SUBLANES),
                                    _pad_rows(s_sc, SUBLANES), w["ab_conv"], w["alog"], w["dtb"],
                                    w["dng"], w["sc_conv"])
    x = _ffn(x, mods[0][2], w["norm_g"][0, 2], w["ff_in"][0, 1], w["ff_out"][0, 1],
             mixer=(y, mods[0][1][2], w["ab_out"]))
    x = _ffn(x, mods[1][0], w["norm_g"][1, 0], w["ff_in"][1, 0], w["ff_out"][1, 0])
    q, kt, vt, ktb, vtb = _sb_qkv(x, mods[1][1], w["norm_g"][1, 1], w["sb_q"], w["sb_kt"], w["sb_vt"])
    o = _sb_prompt(q, ktb, vtb) if cache_t is None else _sb_sample(q, ktb, vtb, cache_t[0], cache_t[1])
    y_out = _ffn(x, mods[1][2], w["norm_g"][1, 2], w["ff_in"][1, 1], w["ff_out"][1, 1],
                 mixer=(o, mods[1][1][2], w["sb_out"]), final_g=w["final_g"])
    new_qkv = conv8[:, SUBLANES - (DN_CONV - 1):]
    new_sc = sc8[:, SUBLANES - (SC_CONV - 1):]
    new_k = jnp.swapaxes(kt, -1, -2)
    new_v = jnp.swapaxes(vt, -1, -2)
    return y_out, new_delta[None], new_qkv[None], new_sc[None], new_k[None], new_v[None]


def kernel(x_prompt, x_sample, c_prompt, c_sample, state_delta, state_qkv_conv, state_sconv, cache_k, cache_v,
           norm_g, ada_w, ada_b, ff_w_in, ff_w_out, ab_w_in, ab_conv_qkv, dn_A_log, dn_dt_bias, dn_norm_g,
           sc_conv, ab_w_out, sb_w_qkv, sb_w_out, final_g):
    depth, _, d = norm_g.shape
    assert depth == 2 and ab_w_in.shape[0] == 1 and sb_w_qkv.shape[0] == 1
    bp, bs = x_prompt.shape[0], x_sample.shape[0]
    heads = dn_A_log.shape[1]
    width = heads * DN_DK
    scw = sc_conv.shape[2]
    sbw = sb_w_qkv.shape[2] // 3
    assert 2 * heads <= LANES

    o_z, o_a, o_s = 3 * width, 4 * width, 4 * width + 2 * heads
    w_in = ab_w_in[0]
    w_sb = sb_w_qkv[0].astype(BF16)
    pad_lane = lambda a: jnp.pad(a, ((0, 0), (0, LANES - a.shape[1])))
    w = {
        "norm_g": norm_g, "final_g": final_g,
        "ff_in": ff_w_in.astype(BF16), "ff_out": ff_w_out.astype(BF16),
        "ab_qkv": w_in[:, :o_z].astype(BF16), "ab_z": w_in[:, o_z:o_a].astype(BF16),
        "ab_ab": pad_lane(w_in[:, o_a:o_s]).astype(BF16), "ab_s": w_in[:, o_s:].astype(BF16),
        "ab_conv": ab_conv_qkv[0], "sc_conv": sc_conv[0],
        "alog": pad_lane(dn_A_log), "dtb": pad_lane(dn_dt_bias),
        "dng": jnp.tile(dn_norm_g[0], heads).reshape(1, width),
        "ab_out": ab_w_out[0].astype(BF16),
        "sb_q": w_sb[:, :sbw], "sb_kt": w_sb[:, sbw:2 * sbw].T, "sb_vt": w_sb[:, 2 * sbw:].T,
        "sb_out": sb_w_out[0].astype(BF16),
    }

    c_all = jnp.concatenate([c_prompt, c_sample], axis=0)
    rows = -(-(bp + bs) // BF16_ROWS) * BF16_ROWS
    c_all = jnp.pad(c_all, ((0, rows - (bp + bs)), (0, 0)))
    mod = _ada_modulation(c_all, ada_w.astype(BF16), ada_b)[:, :bp + bs]
    mod = mod.reshape(depth, bp + bs, N_SUB, 3, 1, d)

    def mods_for(lo, hi):
        return [[tuple(mod[l, lo:hi, s, i] for i in range(3)) for s in range(N_SUB)] for l in range(depth)]

    zeros = lambda shape: jnp.zeros(shape, x_prompt.dtype)
    out_p = _trunk(x_prompt, mods_for(0, bp), zeros((bp, heads, DN_DK, DN_DK)),
                   zeros((bp, DN_CONV - 1, 3 * width)), zeros((bp, SC_CONV - 1, scw)), None, w)
    cache_t = (jnp.swapaxes(cache_k[0], -1, -2), jnp.swapaxes(cache_v[0], -1, -2))
    out_s = _trunk(x_sample, mods_for(bp, bp + bs), state_delta[0], state_qkv_conv[0], state_sconv[0], cache_t, w)
    return (out_p[0], out_s[0]) + out_p[1:] + out_s[1:]
```

```python
import functools
import math

import jax
import jax.numpy as jnp
from jax import lax
from jax.experimental import pallas as pl
from jax.experimental.pallas import tpu as pltpu

F32 = jnp.float32
BF16 = jnp.bfloat16

NORM_EPS = 1e-6
L2_EPS = 1e-6
LOG2_E = math.log2(math.e)
N_SUB = 3
CHUNK = 64
DN_DK = 64
DN_CONV = 4
PRECISE_LEVELS = 5
SC_CONV = 3
SB_DH = 64
ROW_TILE = 512
FF_TILE = 256
SB_BLOCK = 256
SB_CHAINS = 4
SB_CACHE_GROUP = 2
SB_DEAD_LOG2 = -160.0
SUBLANES = 8
BF16_ROWS = 16
LANES = 128
VMEM_LIMIT = 56 * 1024 * 1024


def _params(*semantics):
    return pltpu.CompilerParams(dimension_semantics=semantics, vmem_limit_bytes=VMEM_LIMIT)


def _resident(shape):
    zeros = (0,) * len(shape)
    return pl.BlockSpec(shape, lambda *_: zeros, pipeline_mode=pl.Buffered(1))


def _row_tiling(nseq, t):
    if t >= ROW_TILE:
        assert t % ROW_TILE == 0
        return 1, ROW_TILE
    g = min(nseq, ROW_TILE // t)
    assert nseq % g == 0
    return g, t


def _dot(a, b):
    return jnp.dot(a, b, preferred_element_type=F32)


def _dot_nt(a, b):
    return lax.dot_general(a, b, (((1,), (1,)), ((), ())), preferred_element_type=F32)


def _dot_tn(a, b):
    return lax.dot_general(a, b, (((0,), (0,)), ((), ())), preferred_element_type=F32)


def _split(x):
    hi = x.astype(BF16)
    lo = (x - hi.astype(F32)).astype(BF16)
    return hi, lo


def _dot_x2(x, m_bf16):
    hi, lo = _split(x)
    return _dot(hi, m_bf16) + _dot(lo, m_bf16)


def _split3(x):
    hi = x.astype(BF16)
    r = x - hi.astype(F32)
    mid = r.astype(BF16)
    lo = (r - mid.astype(F32)).astype(BF16)
    return hi, mid, lo


def _dot_x3(x, m_bf16):
    hi, mid, lo = _split3(x)
    return _dot(hi, m_bf16) + (_dot(mid, m_bf16) + _dot(lo, m_bf16))


def _mdot_x3(m_bf16, x):
    hi, mid, lo = _split3(x)
    return _dot(m_bf16, hi) + (_dot(m_bf16, mid) + _dot(m_bf16, lo))


def _dot_p_xp(p, x, with_p):
    ph = p.astype(BF16)
    phf = ph.astype(F32)
    plf = p - phf
    xh, xl = _split(x)
    lhs = jnp.concatenate([phf, plf, phf], axis=1).astype(BF16)
    if with_p:
        yh = jnp.concatenate([xh, ph], axis=1)
        yl = jnp.concatenate([xl, plf.astype(BF16)], axis=1)
    else:
        yh, yl = xh, xl
    return _dot(lhs, jnp.concatenate([yh, yh, yl], axis=0))


def _sigmoid(x):
    return 1.0 / (1.0 + jnp.exp(-x))


def _silu(x):
    return x * _sigmoid(x)


def _softplus(x):
    return jnp.maximum(x, 0.0) + jnp.log(1.0 + jnp.exp(-jnp.abs(x)))


def _rmsnorm(x, g):
    ms = jnp.mean(x * x, axis=-1, keepdims=True)
    return x * lax.rsqrt(ms + NORM_EPS) * g


def _modnorm(x, g, shift, scale):
    return _rmsnorm(x, g) * (1.0 + scale) + shift


def _ada_kernel(c_ref, w_ref, b_ref, o_ref):
    cond = _silu(c_ref[...]).astype(BF16)
    o_ref[0] = _dot(cond, w_ref[0]) + b_ref[0]


def _ada_modulation(c_all, ada_w, ada_b):
    depth, d, n = ada_w.shape
    r = c_all.shape[0]
    tn = 1024 if n % 1024 == 0 else n
    return pl.pallas_call(
        _ada_kernel,
        grid=(depth, n // tn),
        in_specs=[
            pl.BlockSpec((r, d), lambda l, j: (0, 0)),
            pl.BlockSpec((1, d, tn), lambda l, j: (l, 0, j)),
            pl.BlockSpec((1, 1, tn), lambda l, j: (l, 0, j)),
        ],
        out_specs=pl.BlockSpec((1, r, tn), lambda l, j: (l, 0, j)),
        out_shape=jax.ShapeDtypeStruct((depth, r, n), F32),
        compiler_params=_params("parallel", "parallel"),
        name="ada_modulation",
    )(c_all, ada_w, ada_b.reshape(depth, 1, n))


def _ffn_kernel(x_ref, sh_ref, sc_ref, gt_ref, g_ref, win_ref, wout_ref, *rest, mixer, final):
    o_ref = rest[-1]
    x = x_ref[...]
    gg, tt, d = x.shape
    f = wout_ref.shape[0]
    if mixer:
        a_ref, mg_ref, wo_ref = rest[:3]
        rest = rest[3:]
        mix = _dot(a_ref[...].reshape(gg * tt, a_ref.shape[2]), wo_ref[...])
        x = x + mg_ref[...] * mix.reshape(gg, tt, d)
    h = _modnorm(x, g_ref[...], sh_ref[...], sc_ref[...]).astype(BF16).reshape(gg * tt, d)
    acc = jnp.zeros((gg * tt, d), F32)
    for j in range(f // FF_TILE):
        lo = j * FF_TILE
        gate = _dot(h, win_ref[:, lo:lo + FF_TILE])
        up = _dot(h, win_ref[:, f + lo:f + lo + FF_TILE])
        act = (_silu(gate) * up).astype(BF16)
        acc = acc + _dot(act, wout_ref[lo:lo + FF_TILE, :])
    y = x + 0.5 * gt_ref[...] * acc.reshape(gg, tt, d)
    if final:
        y = _rmsnorm(y, rest[0][...])
    o_ref[...] = y


def _ffn(x, mod, g, w_in, w_out, mixer=None, final_g=None):
    nseq, t, d = x.shape
    f = w_out.shape[0]
    assert f % FF_TILE == 0
    gg, tt = _row_tiling(nseq, t)
    shift, scale, gate = mod
    xspec = pl.BlockSpec((gg, tt, d), lambda i, j: (i, j, 0))
    mspec = pl.BlockSpec((gg, 1, d), lambda i, j: (i, 0, 0))
    in_specs = [xspec, mspec, mspec, mspec, _resident((1, d)), _resident((d, 2 * f)), _resident((f, d))]
    args = [x, shift, scale, gate, g.reshape(1, d), w_in, w_out]
    if mixer is not None:
        a, mix_gate, w_o = mixer
        in_specs += [pl.BlockSpec((gg, tt, a.shape[2]), lambda i, j: (i, j, 0)), mspec, _resident(w_o.shape)]
        args += [a, mix_gate, w_o]
    if final_g is not None:
        in_specs.append(_resident((1, d)))
        args.append(final_g.reshape(1, d))
    return pl.pallas_call(
        functools.partial(_ffn_kernel, mixer=mixer is not None, final=final_g is not None),
        grid=(nseq // gg, t // tt),
        in_specs=in_specs,
        out_specs=xspec,
        out_shape=jax.ShapeDtypeStruct(x.shape, F32),
        compiler_params=_params("parallel", "parallel"),
        name="ffn_final" if final_g is not None else "ffn",
    )(*args)


def _proj_kernel(x_ref, sh_ref, sc_ref, g_ref, *refs):
    n_out = len(refs) // 2
    x = x_ref[...]
    gg, tt, d = x.shape
    h = _modnorm(x, g_ref[...], sh_ref[...], sc_ref[...]).astype(BF16).reshape(gg * tt, d)
    for w_ref, o_ref in zip(refs[:n_out], refs[n_out:]):
        o_ref[...] = _dot(h, w_ref[...]).reshape(o_ref.shape).astype(o_ref.dtype)


def _norm_proj(x, mod, g, weights, name):
    nseq, t, d = x.shape
    gg, tt = _row_tiling(nseq, t)
    shift, scale, _ = mod
    xspec = pl.BlockSpec((gg, tt, d), lambda i, j: (i, j, 0))
    mspec = pl.BlockSpec((gg, 1, d), lambda i, j: (i, 0, 0))
    return pl.pallas_call(
        _proj_kernel,
        grid=(nseq // gg, t // tt),
        in_specs=[xspec, mspec, mspec, _resident((1, d))] + [_resident(w.shape) for w in weights],
        out_specs=[pl.BlockSpec((gg, tt, w.shape[1]), lambda i, j: (i, j, 0)) for w in weights],
        out_shape=[jax.ShapeDtypeStruct((nseq, t, w.shape[1]), F32) for w in weights],
        compiler_params=_params("parallel", "parallel"),
        name=name,
    )(x, shift, scale, g.reshape(1, d), *weights)


def _sb_qkv_kernel(x_ref, sh_ref, sc_ref, g_ref, wq_ref, wkt_ref, wvt_ref,
                   q_ref, kt_ref, vt_ref, ktb_ref, vtb_ref):
    x = x_ref[...]
    gg, tt, d = x.shape
    heads = kt_ref.shape[1]
    h = _modnorm(x, g_ref[...], sh_ref[...], sc_ref[...]).astype(BF16).reshape(gg * tt, d)
    q = _dot(h, wq_ref[...]) * (SB_DH ** -0.5 * LOG2_E)
    q_ref[...] = q.reshape(q_ref.shape).astype(q_ref.dtype)
    for w_ref, o_ref, b_ref in ((wkt_ref, kt_ref, ktb_ref), (wvt_ref, vt_ref, vtb_ref)):
        yt = _dot_nt(w_ref[...], h)
        for s in range(gg):
            y = yt[:, s * tt:(s + 1) * tt].reshape(heads, SB_DH, tt)
            o_ref[s] = y
            b_ref[s] = y.astype(b_ref.dtype)


def _sb_qkv(x, mod, g, w_q, w_kt, w_vt):
    nseq, t, d = x.shape
    width = w_q.shape[1]
    heads = width // SB_DH
    gg, tt = _row_tiling(nseq, t)
    shift, scale, _ = mod
    xspec = pl.BlockSpec((gg, tt, d), lambda i, j: (i, j, 0))
    mspec = pl.BlockSpec((gg, 1, d), lambda i, j: (i, 0, 0))
    tspec = pl.BlockSpec((gg, heads, SB_DH, tt), lambda i, j: (i, 0, 0, j))
    tshape = jax.ShapeDtypeStruct((nseq, heads, SB_DH, t), F32)
    return pl.pallas_call(
        _sb_qkv_kernel,
        grid=(nseq // gg, t // tt),
        in_specs=[xspec, mspec, mspec, _resident((1, d)),
                  _resident(w_q.shape), _resident(w_kt.shape), _resident(w_vt.shape)],
        out_specs=[pl.BlockSpec((gg, tt, width), lambda i, j: (i, j, 0)), tspec, tspec, tspec, tspec],
        out_shape=[jax.ShapeDtypeStruct((nseq, t, width), BF16), tshape, tshape,
                   jax.ShapeDtypeStruct(tshape.shape, BF16), jax.ShapeDtypeStruct(tshape.shape, BF16)],
        compiler_params=_params("parallel", "parallel"),
        name="sb_qkv",
    )(x, shift, scale, g.reshape(1, d), w_q, w_kt, w_vt)


def _causal_conv(x, prev8, w):
    taps = w.shape[0]
    full = jnp.concatenate([prev8, x], axis=0)
    y = x * w[taps - 1:taps]
    for j in range(taps - 1):
        y = y + pltpu.roll(full, taps - 1 - j, 0)[SUBLANES:] * w[j:j + 1]
    return y


def _gdn_patterns(rows, heads):
    width = heads * DN_DK
    iota = lambda shape, dim: lax.broadcasted_iota(jnp.int32, shape, dim)
    head_ones = iota((width, width), 0) // DN_DK == iota((width, width), 1) // DN_DK
    rr, cc = iota((rows, rows), 0), iota((rows, rows), 1)
    chunk_ones = rr // CHUNK == cc // CHUNK
    chunk_tril = jnp.logical_and(chunk_ones, cc <= rr)
    li, lh = iota((LANES, width), 0), iota((LANES, width), 1) // DN_DK
    return [m.astype(BF16) for m in (head_ones, chunk_ones, chunk_tril, li == lh, li == lh + heads)]


def _gdn_kernel(qkv_ref, z_ref, ab_ref, s_ref, s0_ref, conv0_ref, sc0_ref,
                wconv_ref, alog_ref, dtb_ref, dng_ref, wsc_ref,
                head_ones_ref, chunk_ones_ref, chunk_tril_ref, expand_g_ref, expand_b_ref,
                y_ref, sout_ref, convout_ref, scout_ref,
                state_ref, convprev_ref, scprev_ref, o_scr):
    step = pl.program_id(1)
    gg, tt, _ = qkv_ref.shape
    heads = state_ref.shape[1]
    width = heads * DN_DK
    scw = wsc_ref.shape[1]
    nc = tt // CHUNK
    nb = gg * nc
    rows = gg * tt

    @pl.when(step == 0)
    def _():
        state_ref[...] = s0_ref[...]
        convprev_ref[...] = conv0_ref[...]
        scprev_ref[...] = sc0_ref[...]

    def per_seq(fn):
        parts = [fn(s) for s in range(gg)]
        return parts[0] if gg == 1 else jnp.concatenate(parts, axis=0)

    wc = wconv_ref[...]
    wsc = wsc_ref[...]
    qkv = _silu(per_seq(lambda s: _causal_conv(qkv_ref[s], convprev_ref[s], wc)))
    cx_parts = [s_ref[s, :, scw:2 * scw] * s_ref[s, :, 2 * scw:] for s in range(gg)]
    y_sc = per_seq(lambda s: s_ref[s, :, :scw] * _causal_conv(cx_parts[s], scprev_ref[s], wsc))
    for s in range(gg):
        convprev_ref[s] = qkv_ref[s, tt - SUBLANES:, :]
        scprev_ref[s] = cx_parts[s][tt - SUBLANES:]
    q, k, v = qkv[:, :width], qkv[:, width:2 * width], qkv[:, 2 * width:]

    head_ones = head_ones_ref[...]
    chunk_ones = chunk_ones_ref[...]
    chunk_tril = chunk_tril_ref[...]
    expand_g = expand_g_ref[...]
    expand_b = expand_b_ref[...]
    fr = lax.broadcasted_iota(jnp.int32, (rows, width), 0) % CHUNK
    ss = lax.broadcasted_iota(jnp.int32, (rows, width), 1) % DN_DK
    incl_all = ss <= fr
    eye_all = ss == fr
    t2 = lax.broadcasted_iota(jnp.int32, (CHUNK, CHUNK), 0)
    s2 = lax.broadcasted_iota(jnp.int32, (CHUNK, CHUNK), 1)
    strict = s2 < t2

    q = q * lax.rsqrt(_dot_x2(q * q, head_ones) + L2_EPS) * (DN_DK ** -0.5)
    k = k * lax.rsqrt(_dot_x2(k * k, head_ones) + L2_EPS)

    ab = per_seq(lambda s: ab_ref[s])
    g_log = -jnp.exp(alog_ref[...]) * _softplus(ab + dtb_ref[...])
    g_cum = _mdot_x3(chunk_tril, g_log)
    gexp = _dot_x3(g_cum, expand_g)
    beta = _dot_x2(_sigmoid(ab), expand_b)
    grow = _mdot_x3(chunk_ones, jnp.where(eye_all, gexp, 0.0))
    decay = jnp.where(incl_all, jnp.exp(jnp.where(incl_all, gexp - grow, 0.0)), 0.0)
    eg = jnp.exp(gexp)
    g3 = gexp.reshape(nb, CHUNK, width)
    g_end = jnp.broadcast_to(g3[:, CHUNK - 1:CHUNK, :], g3.shape).reshape(rows, width)
    k_dec = (k * jnp.exp(g_end - gexp)).astype(BF16)
    eg_end = jnp.exp(g_end)
    bv = beta * v
    bk = beta * eg * k
    qg = eg * q
    q_bf = q.astype(BF16)
    k_bf = k.astype(BF16)

    def blk(a, n, h):
        return a[n * CHUNK:(n + 1) * CHUNK, h * DN_DK:(h + 1) * DN_DK]

    chains = [(n, h) for n in range(nb) for h in range(heads)]
    kk = {c: _dot_nt(blk(k_bf, *c), blk(k_bf, *c)) for c in chains}
    p = {c: -jnp.where(strict, blk(beta, *c) * kk[c] * blk(decay, *c), 0.0) for c in chains}
    x = {c: jnp.concatenate([blk(bv, *c), blk(bk, *c)], axis=1) for c in chains}
    levels = CHUNK.bit_length() - 1
    for lvl in range(levels):
        last = lvl == levels - 1
        res = {}
        for c in chains:
            if lvl < PRECISE_LEVELS:
                res[c] = _dot_p_xp(p[c], x[c], not last)
            else:
                rhs = x[c] if last else jnp.concatenate([x[c], p[c]], axis=1)
                res[c] = _dot(p[c].astype(BF16), rhs.astype(BF16))
        for c in chains:
            x[c] = x[c] + res[c][:, :2 * DN_DK]
            if not last:
                p[c] = res[c][:, 2 * DN_DK:]
    x_bf = {c: x[c].astype(BF16) for c in chains}
    cn = {c: _dot_tn(blk(k_dec, *c), x_bf[c]) for c in chains}
    qk = {c: (_dot_nt(blk(q_bf, *c), blk(k_bf, *c)) * blk(decay, *c)).astype(BF16) for c in chains}
    rp = {c: _dot(qk[c], x_bf[c]) for c in chains}
    pc = {c: jnp.concatenate([blk(qg, *c) - rp[c][:, DN_DK:], cn[c][:, DN_DK:]], axis=0).astype(BF16)
          for c in chains}
    for s in range(gg):
        state = [state_ref[s, h] for h in range(heads)]
        for i in range(nc):
            n = s * nc + i
            ps = [_dot(pc[(n, h)], state[h].astype(BF16)) for h in range(heads)]
            for h in range(heads):
                o_scr[n * CHUNK:(n + 1) * CHUNK, h * DN_DK:(h + 1) * DN_DK] = ps[h][:CHUNK] + rp[(n, h)][:, :DN_DK]
                state[h] = blk(eg_end, n, h) * state[h] - ps[h][CHUNK:] + cn[(n, h)][:, :DN_DK]
        for h in range(heads):
            state_ref[s, h] = state[h]

    o_all = o_scr[...]
    ms = _dot_x2(o_all * o_all, head_ones) * (1.0 / DN_DK)
    z = per_seq(lambda s: z_ref[s])
    o_all = o_all * lax.rsqrt(ms + NORM_EPS) * dng_ref[...] * _silu(z)
    y_ref[:, :, :width] = o_all.reshape(gg, tt, width).astype(y_ref.dtype)
    y_ref[:, :, width:] = y_sc.reshape(gg, tt, scw).astype(y_ref.dtype)

    @pl.when(step == pl.num_programs(1) - 1)
    def _():
        sout_ref[...] = state_ref[...]
        convout_ref[...] = convprev_ref[...]
        scout_ref[...] = scprev_ref[...]


def _gdn(qkv, z, ab, s, s0, conv0, sc0, wconv, alog, dtb, dng, wsc):
    nseq, t, w3 = qkv.shape
    heads = s0.shape[1]
    width = heads * DN_DK
    scw = wsc.shape[1]
    gg, tt = _row_tiling(nseq, t)
    assert tt % CHUNK == 0
    row = lambda n: pl.BlockSpec((gg, tt, n), lambda i, j: (i, j, 0))
    per_seq = lambda shp: pl.BlockSpec((gg,) + shp, lambda i, j: (i,) + (0,) * len(shp))
    patterns = _gdn_patterns(gg * tt, heads)
    return pl.pallas_call(
        _gdn_kernel,
        grid=(nseq // gg, t // tt),
        in_specs=[row(w3), row(width), row(LANES), row(3 * scw),
                  per_seq((heads, DN_DK, DN_DK)), per_seq((SUBLANES, w3)), per_seq((SUBLANES, scw)),
                  _resident(wconv.shape), _resident((1, LANES)), _resident((1, LANES)),
                  _resident((1, width)), _resident(wsc.shape)] + [_resident(m.shape) for m in patterns],
        out_specs=[row(width + scw), per_seq((heads, DN_DK, DN_DK)),
                   per_seq((SUBLANES, w3)), per_seq((SUBLANES, scw))],
        out_shape=[jax.ShapeDtypeStruct((nseq, t, width + scw), BF16),
                   jax.ShapeDtypeStruct((nseq, heads, DN_DK, DN_DK), F32),
                   jax.ShapeDtypeStruct((nseq, SUBLANES, w3), F32),
                   jax.ShapeDtypeStruct((nseq, SUBLANES, scw), F32)],
        scratch_shapes=[pltpu.VMEM((gg, heads, DN_DK, DN_DK), F32),
                        pltpu.VMEM((gg, SUBLANES, w3), F32),
                        pltpu.VMEM((gg, SUBLANES, scw), F32),
                        pltpu.VMEM((gg * tt, width), F32)],
        compiler_params=_params("parallel", "arbitrary"),
        name="gdn_sconv",
    )(qkv, z, ab, s, s0, conv0, sc0, wconv, alog, dtb, dng, wsc, *patterns)


def _sb_blocks(items, upper):
    def run(carries, scale_out):
        n = len(items)
        z, log_sig, log_rest, later, w, scale = ([None] * n for _ in range(6))
        accs, carry_now = {}, dict(carries)

        def stage(i, s):
            chain, q, kt, vt, mask = items[i]
            if s == 0:
                z[i] = _dot(q, kt)
            elif s == 1:
                soft = jnp.log2(1.0 + jnp.exp2(-jnp.abs(z[i])))
                log_sig[i] = jnp.minimum(z[i], 0.0) - soft
                lr = log_sig[i] - z[i]
                log_rest[i] = lr if mask is None else jnp.where(mask, lr, 0.0)
            elif s == 2:
                tk = log_rest[i].shape[1]
                later[i] = _dot(log_rest[i].astype(BF16), upper[:tk, :tk])
            elif s == 3:
                carry = carry_now[chain]
                arg = log_sig[i] + later[i]
                wi = jnp.exp2(arg if scale_out else arg + carry)
                w[i] = (wi if mask is None else jnp.where(mask, wi, 0.0)).astype(BF16)
                scale[i] = jnp.exp2(carry) if scale_out else None
                carry_now[chain] = carry + (later[i][:, 0:1] + log_rest[i][:, 0:1])
            else:
                o = _dot_nt(w[i], vt)
                if scale[i] is not None:
                    o = o * scale[i]
                accs[chain] = o if chain not in accs else accs[chain] + o

        stages = 5
        for tick in range(n + stages - 1):
            for i in range(n):
                if 0 <= tick - i < stages:
                    stage(i, tick - i)
        return {chain: (accs[chain], carry_now[chain]) for chain in accs}
    return run


def _upper(n):
    j = lax.broadcasted_iota(jnp.int32, (n, n), 0)
    s = lax.broadcasted_iota(jnp.int32, (n, n), 1)
    return (j > s).astype(BF16)


def _any_alive(carries):
    top = functools.reduce(jnp.maximum, [jnp.max(c) for c in carries])
    return (top > SB_DEAD_LOG2).astype(jnp.int32)


def _sb_prompt_kernel(q_ref, kt_ref, vt_ref, o_ref):
    hp = kt_ref.shape[1]
    t = q_ref.shape[1]
    blk = min(SB_BLOCK, t)
    upper = _upper(blk)
    tq = lax.broadcasted_iota(jnp.int32, (blk, blk), 0)
    ts = lax.broadcasted_iota(jnp.int32, (blk, blk), 1)
    causal = ts < tq
    zero = jnp.zeros((blk, 1), F32)

    def q_block(qb, has_prev):
        q0 = qb * blk if isinstance(qb, int) else pl.multiple_of(qb * blk, blk)
        qs = [q_ref[0, pl.ds(q0, blk), hh * SB_DH:(hh + 1) * SB_DH] for hh in range(hp)]

        def items(k0, mask):
            return [(hh, qs[hh], kt_ref[0, hh, :, pl.ds(k0, blk)], vt_ref[0, hh, :, pl.ds(k0, blk)], mask)
                    for hh in range(hp)]

        its = items(q0, causal)
        if has_prev:
            its = its + items(pl.multiple_of(q0 - blk, blk), None)
        first = _sb_blocks(its, upper)({hh: zero for hh in range(hp)}, True)
        done = 2 if has_prev else 1

        def k_block(st):
            i, _, oc = st
            k0 = pl.multiple_of((qb - i) * blk, blk)
            res = _sb_blocks(items(k0, None), upper)({hh: oc[hh][1] for hh in range(hp)}, True)
            new = tuple((oc[hh][0] + res[hh][0], res[hh][1]) for hh in range(hp))
            return i + 1, _any_alive([c for _, c in new]), new

        start = tuple(first[hh] for hh in range(hp))
        out = lax.while_loop(lambda st: jnp.logical_and(st[0] <= qb, st[1] > 0), k_block,
                             (jnp.int32(done), _any_alive([c for _, c in start]), start))[2]
        o = jnp.concatenate([out[hh][0] for hh in range(hp)], axis=1)
        o_ref[0, pl.ds(q0, blk), :] = o.astype(o_ref.dtype)

    q_block(0, False)

    def later_q_block(qb, _):
        q_block(qb, True)
        return 0

    lax.fori_loop(1, t // blk, later_q_block, 0)


def _sb_prompt(q, kt, vt):
    b, heads, dh, t = kt.shape
    hp = min(SB_CHAINS, heads)
    assert t % min(SB_BLOCK, t) == 0 and heads % hp == 0 and (hp * dh) % LANES == 0
    qspec = pl.BlockSpec((1, t, hp * dh), lambda i, j: (i, 0, j))
    tspec = pl.BlockSpec((1, hp, dh, t), lambda i, j: (i, j, 0, 0))
    return pl.pallas_call(
        _sb_prompt_kernel,
        grid=(b, heads // hp),
        in_specs=[qspec, tspec, tspec],
        out_specs=qspec,
        out_shape=jax.ShapeDtypeStruct((b, t, heads * dh), BF16),
        compiler_params=_params("parallel", "parallel"),
        name="sb_prompt",
    )(q, kt, vt)


def _sb_sample_kernel(q_ref, kt_ref, vt_ref, ckt_hbm, cvt_hbm, o_ref, kbuf, vbuf, sem):
    hs = kt_ref.shape[1]
    t = q_ref.shape[1]
    past = ckt_hbm.shape[3]
    blk = min(SB_BLOCK, past)
    rows = hs * t
    upper = _upper(max(blk, t))
    nblk = past // blk
    group = math.gcd(nblk, SB_CACHE_GROUP)
    ngroups = nblk // group
    nj = pl.num_programs(1)
    step = pl.program_id(0) * nj + pl.program_id(1)
    older_slot = 2

    def cache_copies(at_step, g, slot):
        k0 = pl.multiple_of(past - (g + 1) * group * blk, blk)
        window = (at_step // nj, pl.ds((at_step % nj) * hs, hs), slice(None), pl.ds(k0, group * blk))
        return (pltpu.make_async_copy(ckt_hbm.at[window], kbuf.at[slot], sem.at[0, slot]),
                pltpu.make_async_copy(cvt_hbm.at[window], vbuf.at[slot], sem.at[1, slot]))

    def start_group(at_step, g, slot):
        for cp in cache_copies(at_step, g, slot):
            cp.start()

    def wait_group(at_step, g, slot):
        for cp in cache_copies(at_step, g, slot):
            cp.wait()

    @pl.when(step == 0)
    def _():
        start_group(step, 0, 0)

    @pl.when(step + 1 < pl.num_programs(0) * nj)
    def _():
        start_group(step + 1, 0, (step + 1) % 2)

    rh = lax.broadcasted_iota(jnp.int32, (rows, hs * SB_DH), 0) // t
    ch = lax.broadcasted_iota(jnp.int32, (rows, hs * SB_DH), 1) // SB_DH
    own = rh == ch
    q4 = q_ref[0]
    q_bd = jnp.where(own, jnp.concatenate([q4] * hs, axis=0), jnp.zeros((), q4.dtype))
    fq = lax.broadcasted_iota(jnp.int32, (rows, t), 0) % t
    fs = lax.broadcasted_iota(jnp.int32, (rows, t), 1)
    causal = fs < fq

    new = (0, q_bd, kt_ref[0].reshape(hs * SB_DH, t), vt_ref[0].reshape(hs * SB_DH, t), causal)

    def cache_items(slot):
        lanes = [slice(r * blk, (r + 1) * blk) for r in reversed(range(group))]
        return [(0, q_bd, kbuf[slot, :, :, ln].reshape(hs * SB_DH, blk).astype(BF16),
                 vbuf[slot, :, :, ln].reshape(hs * SB_DH, blk).astype(BF16), None) for ln in lanes]

    wait_group(step, 0, step % 2)
    acc = _sb_blocks([new] + cache_items(step % 2), upper)({0: jnp.zeros((rows, 1), F32)}, False)[0]

    def older_group(st):
        g, _, oc = st
        start_group(step, g, older_slot)
        wait_group(step, g, older_slot)
        res = _sb_blocks(cache_items(older_slot), upper)({0: oc[1]}, False)[0]
        return g + 1, _any_alive([res[1]]), (oc[0] + res[0], res[1])

    o_full, _ = lax.while_loop(lambda st: jnp.logical_and(st[0] < ngroups, st[1] > 0), older_group,
                               (jnp.int32(1), _any_alive([acc[1]]), acc))[2]

    o_own = jnp.where(own, o_full, 0.0)
    o = o_own[:t]
    for i in range(1, hs):
        o = o + o_own[i * t:(i + 1) * t]
    o_ref[0] = o.astype(o_ref.dtype)


def _sb_sample(q, kt, vt, ckt, cvt):
    b, heads, dh, t = kt.shape
    past = ckt.shape[3]
    hs = SB_BLOCK // t
    blk = min(SB_BLOCK, past)
    assert heads % hs == 0 and past % blk == 0
    group = math.gcd(past // blk, SB_CACHE_GROUP)
    qspec = pl.BlockSpec((1, t, hs * dh), lambda i, j: (i, 0, j))
    tspec = pl.BlockSpec((1, hs, dh, t), lambda i, j: (i, j, 0, 0))
    hbm = pl.BlockSpec(memory_space=pl.ANY)
    slots = 3
    window = pltpu.VMEM((slots, hs, dh, group * blk), ckt.dtype)
    return pl.pallas_call(
        _sb_sample_kernel,
        grid=(b, heads // hs),
        in_specs=[qspec, tspec, tspec, hbm, hbm],
        out_specs=qspec,
        out_shape=jax.ShapeDtypeStruct((b, t, heads * dh), BF16),
        scratch_shapes=[window, window, pltpu.SemaphoreType.DMA((2, slots))],
        compiler_params=_params("arbitrary", "arbitrary"),
        name="sb_sample",
    )(q, kt, vt, ckt, cvt)


def _pad_rows(state, rows):
    return jnp.pad(state, ((0, 0), (rows - state.shape[1], 0), (0, 0)))


def _trunk(x, mods, s_delta, s_qkv, s_sc, cache_t, w):
    x = _ffn(x, mods[0][0], w["norm_g"][0, 0], w["ff_in"][0, 0], w["ff_out"][0, 0])
    qkv, z, ab, s = _norm_proj(x, mods[0][1], w["norm_g"][0, 1],
                               [w["ab_qkv"], w["ab_z"], w["ab_ab"], w["ab_s"]], "ab_in_proj")
    y, new_delta, conv8, sc8 = _gdn(qkv, z, ab, s, s_delta, _pad_rows(s_qkv, SUBLANES),
                                    _pad_rows(s_sc, SUBLANES), w["ab_conv"], w["alog"], w["dtb"],
                                    w["dng"], w["sc_conv"])
    x = _ffn(x, mods[0][2], w["norm_g"][0, 2], w["ff_in"][0, 1], w["ff_out"][0, 1],
             mixer=(y, mods[0][1][2], w["ab_out"]))
    x = _ffn(x, mods[1][0], w["norm_g"][1, 0], w["ff_in"][1, 0], w["ff_out"][1, 0])
    q, kt, vt, ktb, vtb = _sb_qkv(x, mods[1][1], w["norm_g"][1, 1], w["sb_q"], w["sb_kt"], w["sb_vt"])
    o = _sb_prompt(q, ktb, vtb) if cache_t is None else _sb_sample(q, ktb, vtb, cache_t[0], cache_t[1])
    y_out = _ffn(x, mods[1][2], w["norm_g"][1, 2], w["ff_in"][1, 1], w["ff_out"][1, 1],
                 mixer=(o, mods[1][1][2], w["sb_out"]), final_g=w["final_g"])
    new_qkv = conv8[:, SUBLANES - (DN_CONV - 1):]
    new_sc = sc8[:, SUBLANES - (SC_CONV - 1):]
    new_k = jnp.swapaxes(kt, -1, -2)
    new_v = jnp.swapaxes(vt, -1, -2)
    return y_out, new_delta[None], new_qkv[None], new_sc[None], new_k[None], new_v[None]


def kernel(x_prompt, x_sample, c_prompt, c_sample, state_delta, state_qkv_conv, state_sconv, cache_k, cache_v,
           norm_g, ada_w, ada_b, ff_w_in, ff_w_out, ab_w_in, ab_conv_qkv, dn_A_log, dn_dt_bias, dn_norm_g,
           sc_conv, ab_w_out, sb_w_qkv, sb_w_out, final_g):
    depth, _, d = norm_g.shape
    assert depth == 2 and ab_w_in.shape[0] == 1 and sb_w_qkv.shape[0] == 1
    bp, bs = x_prompt.shape[0], x_sample.shape[0]
    heads = dn_A_log.shape[1]
    width = heads * DN_DK
    scw = sc_conv.shape[2]
    sbw = sb_w_qkv.shape[2] // 3
    assert 2 * heads <= LANES

    o_z, o_a, o_s = 3 * width, 4 * width, 4 * width + 2 * heads
    w_in = ab_w_in[0]
    w_sb = sb_w_qkv[0].astype(BF16)
    pad_lane = lambda a: jnp.pad(a, ((0, 0), (0, LANES - a.shape[1])))
    w = {
        "norm_g": norm_g, "final_g": final_g,
        "ff_in": ff_w_in.astype(BF16), "ff_out": ff_w_out.astype(BF16),
        "ab_qkv": w_in[:, :o_z].astype(BF16), "ab_z": w_in[:, o_z:o_a].astype(BF16),
        "ab_ab": pad_lane(w_in[:, o_a:o_s]).astype(BF16), "ab_s": w_in[:, o_s:].astype(BF16),
        "ab_conv": ab_conv_qkv[0], "sc_conv": sc_conv[0],
        "alog": pad_lane(dn_A_log), "dtb": pad_lane(dn_dt_bias),
        "dng": jnp.tile(dn_norm_g[0], heads).reshape(1, width),
        "ab_out": ab_w_out[0].astype(BF16),
        "sb_q": w_sb[:, :sbw], "sb_kt": w_sb[:, sbw:2 * sbw].T, "sb_vt": w_sb[:, 2 * sbw:].T,
        "sb_out": sb_w_out[0].astype(BF16),
    }

    c_all = jnp.concatenate([c_prompt, c_sample], axis=0)
    rows = -(-(bp + bs) // BF16_ROWS) * BF16_ROWS
    c_all = jnp.pad(c_all, ((0, rows - (bp + bs)), (0, 0)))
    mod = _ada_modulation(c_all, ada_w.astype(BF16), ada_b)[:, :bp + bs]
    mod = mod.reshape(depth, bp + bs, N_SUB, 3, 1, d)

    def mods_for(lo, hi):
        return [[tuple(mod[l, lo:hi, s, i] for i in range(3)) for s in range(N_SUB)] for l in range(depth)]

    zeros = lambda shape: jnp.zeros(shape, x_prompt.dtype)
    out_p = _trunk(x_prompt, mods_for(0, bp), zeros((bp, heads, DN_DK, DN_DK)),
                   zeros((bp, DN_CONV - 1, 3 * width)), zeros((bp, SC_CONV - 1, scw)), None, w)
    cache_t = (jnp.swapaxes(cache_k[0], -1, -2), jnp.swapaxes(cache_v[0], -1, -2))
    out_s = _trunk(x_sample, mods_for(bp, bp + bs), state_delta[0], state_qkv_conv[0], state_sconv[0], cache_t, w)
    return (out_p[0], out_s[0]) + out_p[1:] + out_s[1:]
```

```python
import functools
import math

import jax
import jax.numpy as jnp
from jax import lax
from jax.experimental import pallas as pl
from jax.experimental.pallas import tpu as pltpu

F32 = jnp.float32
BF16 = jnp.bfloat16

NORM_EPS = 1e-6
L2_EPS = 1e-6
LOG2_E = math.log2(math.e)
N_SUB = 3
CHUNK = 64
DN_DK = 64
DN_CONV = 4
PRECISE_LEVELS = 5
SC_CONV = 3
SB_DH = 64
ROW_TILE = 512
FF_TILE = 256
SB_BLOCK = 256
SB_CHAINS = 8
SB_CACHE_GROUP = 1
SB_DEAD_LOG2 = -160.0
SUBLANES = 8
BF16_ROWS = 16
LANES = 128
VMEM_LIMIT = 56 * 1024 * 1024


def _params(*semantics):
    return pltpu.CompilerParams(dimension_semantics=semantics, vmem_limit_bytes=VMEM_LIMIT)


def _resident(shape):
    zeros = (0,) * len(shape)
    return pl.BlockSpec(shape, lambda *_: zeros, pipeline_mode=pl.Buffered(1))


def _row_tiling(nseq, t):
    if t >= ROW_TILE:
        assert t % ROW_TILE == 0
        return 1, ROW_TILE
    g = min(nseq, ROW_TILE // t)
    assert nseq % g == 0
    return g, t


def _dot(a, b):
    return jnp.dot(a, b, preferred_element_type=F32)


def _dot_nt(a, b):
    return lax.dot_general(a, b, (((1,), (1,)), ((), ())), preferred_element_type=F32)


def _dot_tn(a, b):
    return lax.dot_general(a, b, (((0,), (0,)), ((), ())), preferred_element_type=F32)


def _split(x):
    hi = x.astype(BF16)
    lo = (x - hi.astype(F32)).astype(BF16)
    return hi, lo


def _dot_x2(x, m_bf16):
    hi, lo = _split(x)
    return _dot(hi, m_bf16) + _dot(lo, m_bf16)


def _split3(x):
    hi = x.astype(BF16)
    r = x - hi.astype(F32)
    mid = r.astype(BF16)
    lo = (r - mid.astype(F32)).astype(BF16)
    return hi, mid, lo


def _dot_x3(x, m_bf16):
    hi, mid, lo = _split3(x)
    return _dot(hi, m_bf16) + (_dot(mid, m_bf16) + _dot(lo, m_bf16))


def _mdot_x3(m_bf16, x):
    hi, mid, lo = _split3(x)
    return _dot(m_bf16, hi) + (_dot(m_bf16, mid) + _dot(m_bf16, lo))


def _dot_p_xp(p, x, with_p):
    ph = p.astype(BF16)
    phf = ph.astype(F32)
    plf = p - phf
    xh, xl = _split(x)
    lhs = jnp.concatenate([phf, plf, phf], axis=1).astype(BF16)
    if with_p:
        yh = jnp.concatenate([xh, ph], axis=1)
        yl = jnp.concatenate([xl, plf.astype(BF16)], axis=1)
    else:
        yh, yl = xh, xl
    return _dot(lhs, jnp.concatenate([yh, yh, yl], axis=0))


def _sigmoid(x):
    return 1.0 / (1.0 + jnp.exp(-x))


def _silu(x):
    return x * _sigmoid(x)


def _softplus(x):
    return jnp.maximum(x, 0.0) + jnp.log(1.0 + jnp.exp(-jnp.abs(x)))


def _rmsnorm(x, g):
    ms = jnp.mean(x * x, axis=-1, keepdims=True)
    return x * lax.rsqrt(ms + NORM_EPS) * g


def _modnorm(x, g, shift, scale):
    return _rmsnorm(x, g) * (1.0 + scale) + shift


def _ada_kernel(c_ref, w_ref, b_ref, o_ref):
    cond = _silu(c_ref[...]).astype(BF16)
    o_ref[0] = _dot(cond, w_ref[0]) + b_ref[0]


def _ada_modulation(c_all, ada_w, ada_b):
    depth, d, n = ada_w.shape
    r = c_all.shape[0]
    tn = 1024 if n % 1024 == 0 else n
    return pl.pallas_call(
        _ada_kernel,
        grid=(depth, n // tn),
        in_specs=[
            pl.BlockSpec((r, d), lambda l, j: (0, 0)),
            pl.BlockSpec((1, d, tn), lambda l, j: (l, 0, j)),
            pl.BlockSpec((1, 1, tn), lambda l, j: (l, 0, j)),
        ],
        out_specs=pl.BlockSpec((1, r, tn), lambda l, j: (l, 0, j)),
        out_shape=jax.ShapeDtypeStruct((depth, r, n), F32),
        compiler_params=_params("parallel", "parallel"),
        name="ada_modulation",
    )(c_all, ada_w, ada_b.reshape(depth, 1, n))


def _ffn_kernel(x_ref, sh_ref, sc_ref, gt_ref, g_ref, win_ref, wout_ref, *rest, mixer, final):
    o_ref = rest[-1]
    x = x_ref[...]
    gg, tt, d = x.shape
    f = wout_ref.shape[0]
    if mixer:
        a_ref, mg_ref, wo_ref = rest[:3]
        rest = rest[3:]
        mix = _dot(a_ref[...].reshape(gg * tt, a_ref.shape[2]), wo_ref[...])
        x = x + mg_ref[...] * mix.reshape(gg, tt, d)
    h = _modnorm(x, g_ref[...], sh_ref[...], sc_ref[...]).astype(BF16).reshape(gg * tt, d)
    acc = jnp.zeros((gg * tt, d), F32)
    for j in range(f // FF_TILE):
        lo = j * FF_TILE
        gate = _dot(h, win_ref[:, lo:lo + FF_TILE])
        up = _dot(h, win_ref[:, f + lo:f + lo + FF_TILE])
        act = (_silu(gate) * up).astype(BF16)
        acc = acc + _dot(act, wout_ref[lo:lo + FF_TILE, :])
    y = x + 0.5 * gt_ref[...] * acc.reshape(gg, tt, d)
    if final:
        y = _rmsnorm(y, rest[0][...])
    o_ref[...] = y


def _ffn(x, mod, g, w_in, w_out, mixer=None, final_g=None):
    nseq, t, d = x.shape
    f = w_out.shape[0]
    assert f % FF_TILE == 0
    gg, tt = _row_tiling(nseq, t)
    shift, scale, gate = mod
    xspec = pl.BlockSpec((gg, tt, d), lambda i, j: (i, j, 0))
    mspec = pl.BlockSpec((gg, 1, d), lambda i, j: (i, 0, 0))
    in_specs = [xspec, mspec, mspec, mspec, _resident((1, d)), _resident((d, 2 * f)), _resident((f, d))]
    args = [x, shift, scale, gate, g.reshape(1, d), w_in, w_out]
    if mixer is not None:
        a, mix_gate, w_o = mixer
        in_specs += [pl.BlockSpec((gg, tt, a.shape[2]), lambda i, j: (i, j, 0)), mspec, _resident(w_o.shape)]
        args += [a, mix_gate, w_o]
    if final_g is not None:
        in_specs.append(_resident((1, d)))
        args.append(final_g.reshape(1, d))
    return pl.pallas_call(
        functools.partial(_ffn_kernel, mixer=mixer is not None, final=final_g is not None),
        grid=(nseq // gg, t // tt),
        in_specs=in_specs,
        out_specs=xspec,
        out_shape=jax.ShapeDtypeStruct(x.shape, F32),
        compiler_params=_params("parallel", "parallel"),
        name="ffn_final" if final_g is not None else "ffn",
    )(*args)


def _proj_kernel(x_ref, sh_ref, sc_ref, g_ref, *refs):
    n_out = len(refs) // 2
    x = x_ref[...]
    gg, tt, d = x.shape
    h = _modnorm(x, g_ref[...], sh_ref[...], sc_ref[...]).astype(BF16).reshape(gg * tt, d)
    for w_ref, o_ref in zip(refs[:n_out], refs[n_out:]):
        o_ref[...] = _dot(h, w_ref[...]).reshape(o_ref.shape).astype(o_ref.dtype)


def _norm_proj(x, mod, g, weights, name):
    nseq, t, d = x.shape
    gg, tt = _row_tiling(nseq, t)
    shift, scale, _ = mod
    xspec = pl.BlockSpec((gg, tt, d), lambda i, j: (i, j, 0))
    mspec = pl.BlockSpec((gg, 1, d), lambda i, j: (i, 0, 0))
    return pl.pallas_call(
        _proj_kernel,
        grid=(nseq // gg, t // tt),
        in_specs=[xspec, mspec, mspec, _resident((1, d))] + [_resident(w.shape) for w in weights],
        out_specs=[pl.BlockSpec((gg, tt, w.shape[1]), lambda i, j: (i, j, 0)) for w in weights],
        out_shape=[jax.ShapeDtypeStruct((nseq, t, w.shape[1]), F32) for w in weights],
        compiler_params=_params("parallel", "parallel"),
        name=name,
    )(x, shift, scale, g.reshape(1, d), *weights)


def _sb_qkv_kernel(x_ref, sh_ref, sc_ref, g_ref, wq_ref, wkt_ref, wvt_ref,
                   q_ref, kt_ref, vt_ref, ktb_ref, vtb_ref):
    x = x_ref[...]
    gg, tt, d = x.shape
    heads = kt_ref.shape[1]
    h = _modnorm(x, g_ref[...], sh_ref[...], sc_ref[...]).astype(BF16).reshape(gg * tt, d)
    q = _dot(h, wq_ref[...]) * (SB_DH ** -0.5 * LOG2_E)
    q_ref[...] = q.reshape(q_ref.shape).astype(q_ref.dtype)
    for w_ref, o_ref, b_ref in ((wkt_ref, kt_ref, ktb_ref), (wvt_ref, vt_ref, vtb_ref)):
        yt = _dot_nt(w_ref[...], h)
        for s in range(gg):
            y = yt[:, s * tt:(s + 1) * tt].reshape(heads, SB_DH, tt)
            o_ref[s] = y
            b_ref[s] = y.astype(b_ref.dtype)


def _sb_qkv(x, mod, g, w_q, w_kt, w_vt):
    nseq, t, d = x.shape
    width = w_q.shape[1]
    heads = width // SB_DH
    gg, tt = _row_tiling(nseq, t)
    shift, scale, _ = mod
    xspec = pl.BlockSpec((gg, tt, d), lambda i, j: (i, j, 0))
    mspec = pl.BlockSpec((gg, 1, d), lambda i, j: (i, 0, 0))
    tspec = pl.BlockSpec((gg, heads, SB_DH, tt), lambda i, j: (i, 0, 0, j))
    tshape = jax.ShapeDtypeStruct((nseq, heads, SB_DH, t), F32)
    return pl.pallas_call(
        _sb_qkv_kernel,
        grid=(nseq // gg, t // tt),
        in_specs=[xspec, mspec, mspec, _resident((1, d)),
                  _resident(w_q.shape), _resident(w_kt.shape), _resident(w_vt.shape)],
        out_specs=[pl.BlockSpec((gg, tt, width), lambda i, j: (i, j, 0)), tspec, tspec, tspec, tspec],
        out_shape=[jax.ShapeDtypeStruct((nseq, t, width), BF16), tshape, tshape,
                   jax.ShapeDtypeStruct(tshape.shape, BF16), jax.ShapeDtypeStruct(tshape.shape, BF16)],
        compiler_params=_params("parallel", "parallel"),
        name="sb_qkv",
    )(x, shift, scale, g.reshape(1, d), w_q, w_kt, w_vt)


def _causal_conv(x, prev8, w):
    taps = w.shape[0]
    full = jnp.concatenate([prev8, x], axis=0)
    y = x * w[taps - 1:taps]
    for j in range(taps - 1):
        y = y + pltpu.roll(full, taps - 1 - j, 0)[SUBLANES:] * w[j:j + 1]
    return y


def _gdn_patterns(rows, heads):
    width = heads * DN_DK
    iota = lambda shape, dim: lax.broadcasted_iota(jnp.int32, shape, dim)
    head_ones = iota((width, width), 0) // DN_DK == iota((width, width), 1) // DN_DK
    rr, cc = iota((rows, rows), 0), iota((rows, rows), 1)
    chunk_ones = rr // CHUNK == cc // CHUNK
    chunk_tril = jnp.logical_and(chunk_ones, cc <= rr)
    li, lh = iota((LANES, width), 0), iota((LANES, width), 1) // DN_DK
    return [m.astype(BF16) for m in (head_ones, chunk_ones, chunk_tril, li == lh, li == lh + heads)]


def _gdn_kernel(qkv_ref, z_ref, ab_ref, s_ref, s0_ref, conv0_ref, sc0_ref,
                wconv_ref, alog_ref, dtb_ref, dng_ref, wsc_ref,
                head_ones_ref, chunk_ones_ref, chunk_tril_ref, expand_g_ref, expand_b_ref,
                y_ref, sout_ref, convout_ref, scout_ref,
                state_ref, convprev_ref, scprev_ref, o_scr):
    step = pl.program_id(1)
    gg, tt, _ = qkv_ref.shape
    heads = state_ref.shape[1]
    width = heads * DN_DK
    scw = wsc_ref.shape[1]
    nc = tt // CHUNK
    nb = gg * nc
    rows = gg * tt

    @pl.when(step == 0)
    def _():
        state_ref[...] = s0_ref[...]
        convprev_ref[...] = conv0_ref[...]
        scprev_ref[...] = sc0_ref[...]

    def per_seq(fn):
        parts = [fn(s) for s in range(gg)]
        return parts[0] if gg == 1 else jnp.concatenate(parts, axis=0)

    wc = wconv_ref[...]
    wsc = wsc_ref[...]
    qkv = _silu(per_seq(lambda s: _causal_conv(qkv_ref[s], convprev_ref[s], wc)))
    cx_parts = [s_ref[s, :, scw:2 * scw] * s_ref[s, :, 2 * scw:] for s in range(gg)]
    y_sc = per_seq(lambda s: s_ref[s, :, :scw] * _causal_conv(cx_parts[s], scprev_ref[s], wsc))
    for s in range(gg):
        convprev_ref[s] = qkv_ref[s, tt - SUBLANES:, :]
        scprev_ref[s] = cx_parts[s][tt - SUBLANES:]
    q, k, v = qkv[:, :width], qkv[:, width:2 * width], qkv[:, 2 * width:]

    head_ones = head_ones_ref[...]
    chunk_ones = chunk_ones_ref[...]
    chunk_tril = chunk_tril_ref[...]
    expand_g = expand_g_ref[...]
    expand_b = expand_b_ref[...]
    fr = lax.broadcasted_iota(jnp.int32, (rows, width), 0) % CHUNK
    ss = lax.broadcasted_iota(jnp.int32, (rows, width), 1) % DN_DK
    incl_all = ss <= fr
    eye_all = ss == fr
    t2 = lax.broadcasted_iota(jnp.int32, (CHUNK, CHUNK), 0)
    s2 = lax.broadcasted_iota(jnp.int32, (CHUNK, CHUNK), 1)
    strict = s2 < t2

    q = q * lax.rsqrt(_dot_x2(q * q, head_ones) + L2_EPS) * (DN_DK ** -0.5)
    k = k * lax.rsqrt(_dot_x2(k * k, head_ones) + L2_EPS)

    ab = per_seq(lambda s: ab_ref[s])
    g_log = -jnp.exp(alog_ref[...]) * _softplus(ab + dtb_ref[...])
    g_cum = _mdot_x3(chunk_tril, g_log)
    gexp = _dot_x3(g_cum, expand_g)
    beta = _dot_x2(_sigmoid(ab), expand_b)
    grow = _mdot_x3(chunk_ones, jnp.where(eye_all, gexp, 0.0))
    decay = jnp.where(incl_all, jnp.exp(jnp.where(incl_all, gexp - grow, 0.0)), 0.0)
    eg = jnp.exp(gexp)
    g3 = gexp.reshape(nb, CHUNK, width)
    g_end = jnp.broadcast_to(g3[:, CHUNK - 1:CHUNK, :], g3.shape).reshape(rows, width)
    k_dec = (k * jnp.exp(g_end - gexp)).astype(BF16)
    eg_end = jnp.exp(g_end)
    bv = beta * v
    bk = beta * eg * k
    qg = eg * q
    q_bf = q.astype(BF16)
    k_bf = k.astype(BF16)

    def blk(a, n, h):
        return a[n * CHUNK:(n + 1) * CHUNK, h * DN_DK:(h + 1) * DN_DK]

    chains = [(n, h) for n in range(nb) for h in range(heads)]
    kk = {c: _dot_nt(blk(k_bf, *c), blk(k_bf, *c)) for c in chains}
    p = {c: -jnp.where(strict, blk(beta, *c) * kk[c] * blk(decay, *c), 0.0) for c in chains}
    x = {c: jnp.concatenate([blk(bv, *c), blk(bk, *c)], axis=1) for c in chains}
    levels = CHUNK.bit_length() - 1
    for lvl in range(levels):
        last = lvl == levels - 1
        res = {}
        for c in chains:
            if lvl < PRECISE_LEVELS:
                res[c] = _dot_p_xp(p[c], x[c], not last)
            else:
                rhs = x[c] if last else jnp.concatenate([x[c], p[c]], axis=1)
                res[c] = _dot(p[c].astype(BF16), rhs.astype(BF16))
        for c in chains:
            x[c] = x[c] + res[c][:, :2 * DN_DK]
            if not last:
                p[c] = res[c][:, 2 * DN_DK:]
    x_bf = {c: x[c].astype(BF16) for c in chains}
    cn = {c: _dot_tn(blk(k_dec, *c), x_bf[c]) for c in chains}
    qk = {c: (_dot_nt(blk(q_bf, *c), blk(k_bf, *c)) * blk(decay, *c)).astype(BF16) for c in chains}
    rp = {c: _dot(qk[c], x_bf[c]) for c in chains}
    pc = {c: jnp.concatenate([blk(qg, *c) - rp[c][:, DN_DK:], cn[c][:, DN_DK:]], axis=0).astype(BF16)
          for c in chains}
    for s in range(gg):
        state = [state_ref[s, h] for h in range(heads)]
        for i in range(nc):
            n = s * nc + i
            ps = [_dot(pc[(n, h)], state[h].astype(BF16)) for h in range(heads)]
            for h in range(heads):
                o_scr[n * CHUNK:(n + 1) * CHUNK, h * DN_DK:(h + 1) * DN_DK] = ps[h][:CHUNK] + rp[(n, h)][:, :DN_DK]
                state[h] = blk(eg_end, n, h) * state[h] - ps[h][CHUNK:] + cn[(n, h)][:, :DN_DK]
        for h in range(heads):
            state_ref[s, h] = state[h]

    o_all = o_scr[...]
    ms = _dot_x2(o_all * o_all, head_ones) * (1.0 / DN_DK)
    z = per_seq(lambda s: z_ref[s])
    o_all = o_all * lax.rsqrt(ms + NORM_EPS) * dng_ref[...] * _silu(z)
    y_ref[:, :, :width] = o_all.reshape(gg, tt, width).astype(y_ref.dtype)
    y_ref[:, :, width:] = y_sc.reshape(gg, tt, scw).astype(y_ref.dtype)

    @pl.when(step == pl.num_programs(1) - 1)
    def _():
        sout_ref[...] = state_ref[...]
        convout_ref[...] = convprev_ref[...]
        scout_ref[...] = scprev_ref[...]


def _gdn(qkv, z, ab, s, s0, conv0, sc0, wconv, alog, dtb, dng, wsc):
    nseq, t, w3 = qkv.shape
    heads = s0.shape[1]
    width = heads * DN_DK
    scw = wsc.shape[1]
    gg, tt = _row_tiling(nseq, t)
    assert tt % CHUNK == 0
    row = lambda n: pl.BlockSpec((gg, tt, n), lambda i, j: (i, j, 0))
    per_seq = lambda shp: pl.BlockSpec((gg,) + shp, lambda i, j: (i,) + (0,) * len(shp))
    patterns = _gdn_patterns(gg * tt, heads)
    return pl.pallas_call(
        _gdn_kernel,
        grid=(nseq // gg, t // tt),
        in_specs=[row(w3), row(width), row(LANES), row(3 * scw),
                  per_seq((heads, DN_DK, DN_DK)), per_seq((SUBLANES, w3)), per_seq((SUBLANES, scw)),
                  _resident(wconv.shape), _resident((1, LANES)), _resident((1, LANES)),
                  _resident((1, width)), _resident(wsc.shape)] + [_resident(m.shape) for m in patterns],
        out_specs=[row(width + scw), per_seq((heads, DN_DK, DN_DK)),
                   per_seq((SUBLANES, w3)), per_seq((SUBLANES, scw))],
        out_shape=[jax.ShapeDtypeStruct((nseq, t, width + scw), BF16),
                   jax.ShapeDtypeStruct((nseq, heads, DN_DK, DN_DK), F32),
                   jax.ShapeDtypeStruct((nseq, SUBLANES, w3), F32),
                   jax.ShapeDtypeStruct((nseq, SUBLANES, scw), F32)],
        scratch_shapes=[pltpu.VMEM((gg, heads, DN_DK, DN_DK), F32),
                        pltpu.VMEM((gg, SUBLANES, w3), F32),
                        pltpu.VMEM((gg, SUBLANES, scw), F32),
                        pltpu.VMEM((gg * tt, width), F32)],
        compiler_params=_params("parallel", "arbitrary"),
        name="gdn_sconv",
    )(qkv, z, ab, s, s0, conv0, sc0, wconv, alog, dtb, dng, wsc, *patterns)


def _sb_blocks(items, upper):
    def run(carries, scale_out):
        n = len(items)
        z, log_sig, log_rest, later, w, scale = ([None] * n for _ in range(6))
        accs, carry_now = {}, dict(carries)

        def stage(i, s):
            chain, q, kt, vt, mask = items[i]
            if s == 0:
                z[i] = _dot(q, kt)
            elif s == 1:
                soft = jnp.log2(1.0 + jnp.exp2(-jnp.abs(z[i])))
                log_sig[i] = jnp.minimum(z[i], 0.0) - soft
                lr = log_sig[i] - z[i]
                log_rest[i] = lr if mask is None else jnp.where(mask, lr, 0.0)
            elif s == 2:
                tk = log_rest[i].shape[1]
                later[i] = _dot(log_rest[i].astype(BF16), upper[:tk, :tk])
            elif s == 3:
                carry = carry_now[chain]
                arg = log_sig[i] + later[i]
                wi = jnp.exp2(arg if scale_out else arg + carry)
                w[i] = (wi if mask is None else jnp.where(mask, wi, 0.0)).astype(BF16)
                scale[i] = jnp.exp2(carry) if scale_out else None
                carry_now[chain] = carry + (later[i][:, 0:1] + log_rest[i][:, 0:1])
            else:
                o = _dot_nt(w[i], vt)
                if scale[i] is not None:
                    o = o * scale[i]
                accs[chain] = o if chain not in accs else accs[chain] + o

        stages = 5
        for tick in range(n + stages - 1):
            for i in range(n):
                if 0 <= tick - i < stages:
                    stage(i, tick - i)
        return {chain: (accs[chain], carry_now[chain]) for chain in accs}
    return run


def _upper(n):
    j = lax.broadcasted_iota(jnp.int32, (n, n), 0)
    s = lax.broadcasted_iota(jnp.int32, (n, n), 1)
    return (j > s).astype(BF16)


def _any_alive(carries):
    top = functools.reduce(jnp.maximum, [jnp.max(c) for c in carries])
    return (top > SB_DEAD_LOG2).astype(jnp.int32)


def _sb_prompt_kernel(q_ref, kt_ref, vt_ref, o_ref):
    hp = kt_ref.shape[1]
    t = q_ref.shape[1]
    blk = min(SB_BLOCK, t)
    upper = _upper(blk)
    tq = lax.broadcasted_iota(jnp.int32, (blk, blk), 0)
    ts = lax.broadcasted_iota(jnp.int32, (blk, blk), 1)
    causal = ts < tq
    zero = jnp.zeros((blk, 1), F32)

    def q_block(qb, has_prev):
        q0 = qb * blk if isinstance(qb, int) else pl.multiple_of(qb * blk, blk)
        qs = [q_ref[0, pl.ds(q0, blk), hh * SB_DH:(hh + 1) * SB_DH] for hh in range(hp)]

        def items(k0, mask):
            return [(hh, qs[hh], kt_ref[0, hh, :, pl.ds(k0, blk)], vt_ref[0, hh, :, pl.ds(k0, blk)], mask)
                    for hh in range(hp)]

        its = items(q0, causal)
        if has_prev:
            its = its + items(pl.multiple_of(q0 - blk, blk), None)
        first = _sb_blocks(its, upper)({hh: zero for hh in range(hp)}, True)
        done = 2 if has_prev else 1

        def k_block(st):
            i, _, oc = st
            k0 = pl.multiple_of((qb - i) * blk, blk)
            res = _sb_blocks(items(k0, None), upper)({hh: oc[hh][1] for hh in range(hp)}, True)
            new = tuple((oc[hh][0] + res[hh][0], res[hh][1]) for hh in range(hp))
            return i + 1, _any_alive([c for _, c in new]), new

        start = tuple(first[hh] for hh in range(hp))
        out = lax.while_loop(lambda st: jnp.logical_and(st[0] <= qb, st[1] > 0), k_block,
                             (jnp.int32(done), _any_alive([c for _, c in start]), start))[2]
        o = jnp.concatenate([out[hh][0] for hh in range(hp)], axis=1)
        o_ref[0, pl.ds(q0, blk), :] = o.astype(o_ref.dtype)

    q_block(0, False)

    def later_q_block(qb, _):
        q_block(qb, True)
        return 0

    lax.fori_loop(1, t // blk, later_q_block, 0)


def _sb_prompt(q, kt, vt):
    b, heads, dh, t = kt.shape
    hp = min(SB_CHAINS, heads)
    assert t % min(SB_BLOCK, t) == 0 and heads % hp == 0 and (hp * dh) % LANES == 0
    qspec = pl.BlockSpec((1, t, hp * dh), lambda i, j: (i, 0, j))
    tspec = pl.BlockSpec((1, hp, dh, t), lambda i, j: (i, j, 0, 0))
    return pl.pallas_call(
        _sb_prompt_kernel,
        grid=(b, heads // hp),
        in_specs=[qspec, tspec, tspec],
        out_specs=qspec,
        out_shape=jax.ShapeDtypeStruct((b, t, heads * dh), BF16),
        compiler_params=_params("parallel", "parallel"),
        name="sb_prompt",
    )(q, kt, vt)


def _sb_sample_kernel(q_ref, kt_ref, vt_ref, ckt_hbm, cvt_hbm, o_ref, kbuf, vbuf, sem):
    hs = kt_ref.shape[1]
    t = q_ref.shape[1]
    past = ckt_hbm.shape[3]
    blk = min(SB_BLOCK, past)
    rows = hs * t
    upper = _upper(max(blk, t))
    nblk = past // blk
    group = math.gcd(nblk, SB_CACHE_GROUP)
    ngroups = nblk // group
    nj = pl.num_programs(1)
    step = pl.program_id(0) * nj + pl.program_id(1)
    older_slot = 2

    def cache_copies(at_step, g, slot):
        k0 = pl.multiple_of(past - (g + 1) * group * blk, blk)
        window = (at_step // nj, pl.ds((at_step % nj) * hs, hs), slice(None), pl.ds(k0, group * blk))
        return (pltpu.make_async_copy(ckt_hbm.at[window], kbuf.at[slot], sem.at[0, slot]),
                pltpu.make_async_copy(cvt_hbm.at[window], vbuf.at[slot], sem.at[1, slot]))

    def start_group(at_step, g, slot):
        for cp in cache_copies(at_step, g, slot):
            cp.start()

    def wait_group(at_step, g, slot):
        for cp in cache_copies(at_step, g, slot):
            cp.wait()

    @pl.when(step == 0)
    def _():
        start_group(step, 0, 0)

    @pl.when(step + 1 < pl.num_programs(0) * nj)
    def _():
        start_group(step + 1, 0, (step + 1) % 2)

    rh = lax.broadcasted_iota(jnp.int32, (rows, hs * SB_DH), 0) // t
    ch = lax.broadcasted_iota(jnp.int32, (rows, hs * SB_DH), 1) // SB_DH
    own = rh == ch
    q4 = q_ref[0]
    q_bd = jnp.where(own, jnp.concatenate([q4] * hs, axis=0), jnp.zeros((), q4.dtype))
    fq = lax.broadcasted_iota(jnp.int32, (rows, t), 0) % t
    fs = lax.broadcasted_iota(jnp.int32, (rows, t), 1)
    causal = fs < fq

    new = (0, q_bd, kt_ref[0].reshape(hs * SB_DH, t), vt_ref[0].reshape(hs * SB_DH, t), causal)

    def cache_items(slot):
        lanes = [slice(r * blk, (r + 1) * blk) for r in reversed(range(group))]
        return [(0, q_bd, kbuf[slot, :, :, ln].reshape(hs * SB_DH, blk).astype(BF16),
                 vbuf[slot, :, :, ln].reshape(hs * SB_DH, blk).astype(BF16), None) for ln in lanes]

    wait_group(step, 0, step % 2)
    acc = _sb_blocks([new] + cache_items(step % 2), upper)({0: jnp.zeros((rows, 1), F32)}, False)[0]

    def older_group(st):
        g, _, oc = st
        start_group(step, g, older_slot)
        wait_group(step, g, older_slot)
        res = _sb_blocks(cache_items(older_slot), upper)({0: oc[1]}, False)[0]
        return g + 1, _any_alive([res[1]]), (oc[0] + res[0], res[1])

    o_full, _ = lax.while_loop(lambda st: jnp.logical_and(st[0] < ngroups, st[1] > 0), older_group,
                               (jnp.int32(1), _any_alive([acc[1]]), acc))[2]

    o_own = jnp.where(own, o_full, 0.0)
    o = o_own[:t]
    for i in range(1, hs):
        o = o + o_own[i * t:(i + 1) * t]
    o_ref[0] = o.astype(o_ref.dtype)


def _sb_sample(q, kt, vt, ckt, cvt):
    b, heads, dh, t = kt.shape
    past = ckt.shape[3]
    hs = SB_BLOCK // t
    blk = min(SB_BLOCK, past)
    assert heads % hs == 0 and past % blk == 0
    group = math.gcd(past // blk, SB_CACHE_GROUP)
    qspec = pl.BlockSpec((1, t, hs * dh), lambda i, j: (i, 0, j))
    tspec = pl.BlockSpec((1, hs, dh, t), lambda i, j: (i, j, 0, 0))
    hbm = pl.BlockSpec(memory_space=pl.ANY)
    slots = 3
    window = pltpu.VMEM((slots, hs, dh, group * blk), ckt.dtype)
    return pl.pallas_call(
        _sb_sample_kernel,
        grid=(b, heads // hs),
        in_specs=[qspec, tspec, tspec, hbm, hbm],
        out_specs=qspec,
        out_shape=jax.ShapeDtypeStruct((b, t, heads * dh), BF16),
        scratch_shapes=[window, window, pltpu.SemaphoreType.DMA((2, slots))],
        compiler_params=_params("arbitrary", "arbitrary"),
        name="sb_sample",
    )(q, kt, vt, ckt, cvt)


def _pad_rows(state, rows):
    return jnp.pad(state, ((0, 0), (rows - state.shape[1], 0), (0, 0)))


def _trunk(x, mods, s_delta, s_qkv, s_sc, cache_t, w):
    x = _ffn(x, mods[0][0], w["norm_g"][0, 0], w["ff_in"][0, 0], w["ff_out"][0, 0])
    qkv, z, ab, s = _norm_proj(x, mods[0][1], w["norm_g"][0, 1],
                               [w["ab_qkv"], w["ab_z"], w["ab_ab"], w["ab_s"]], "ab_in_proj")
    y, new_delta, conv8, sc8 = _gdn(qkv, z, ab, s, s_delta, _pad_rows(s_qkv, SUBLANES),
                                    _pad_rows(s_sc, SUBLANES), w["ab_conv"], w["alog"], w["dtb"],
                                    w["dng"], w["sc_conv"])
    x = _ffn(x, mods[0][2], w["norm_g"][0, 2], w["ff_in"][0, 1], w["ff_out"][0, 1],
             mixer=(y, mods[0][1][2], w["ab_out"]))
    x = _ffn(x, mods[1][0], w["norm_g"][1, 0], w["ff_in"][1, 0], w["ff_out"][1, 0])
    q, kt, vt, ktb, vtb = _sb_qkv(x, mods[1][1], w["norm_g"][1, 1], w["sb_q"], w["sb_kt"], w["sb_vt"])
    o = _sb_prompt(q, ktb, vtb) if cache_t is None else _sb_sample(q, ktb, vtb, cache_t[0], cache_t[1])
    y_out = _ffn(x, mods[1][2], w["norm_g"][1, 2], w["ff_in"][1, 1], w["ff_out"][1, 1],
                 mixer=(o, mods[1][1][2], w["sb_out"]), final_g=w["final_g"])
    new_qkv = conv8[:, SUBLANES - (DN_CONV - 1):]
    new_sc = sc8[:, SUBLANES - (SC_CONV - 1):]
    new_k = jnp.swapaxes(kt, -1, -2)
    new_v = jnp.swapaxes(vt, -1, -2)
    return y_out, new_delta[None], new_qkv[None], new_sc[None], new_k[None], new_v[None]


def kernel(x_prompt, x_sample, c_prompt, c_sample, state_delta, state_qkv_conv, state_sconv, cache_k, cache_v,
           norm_g, ada_w, ada_b, ff_w_in, ff_w_out, ab_w_in, ab_conv_qkv, dn_A_log, dn_dt_bias, dn_norm_g,
           sc_conv, ab_w_out, sb_w_qkv, sb_w_out, final_g):
    depth, _, d = norm_g.shape
    assert depth == 2 and ab_w_in.shape[0] == 1 and sb_w_qkv.shape[0] == 1
    bp, bs = x_prompt.shape[0], x_sample.shape[0]
    heads = dn_A_log.shape[1]
    width = heads * DN_DK
    scw = sc_conv.shape[2]
    sbw = sb_w_qkv.shape[2] // 3
    assert 2 * heads <= LANES

    o_z, o_a, o_s = 3 * width, 4 * width, 4 * width + 2 * heads
    w_in = ab_w_in[0]
    w_sb = sb_w_qkv[0].astype(BF16)
    pad_lane = lambda a: jnp.pad(a, ((0, 0), (0, LANES - a.shape[1])))
    w = {
        "norm_g": norm_g, "final_g": final_g,
        "ff_in": ff_w_in.astype(BF16), "ff_out": ff_w_out.astype(BF16),
        "ab_qkv": w_in[:, :o_z].astype(BF16), "ab_z": w_in[:, o_z:o_a].astype(BF16),
        "ab_ab": pad_lane(w_in[:, o_a:o_s]).astype(BF16), "ab_s": w_in[:, o_s:].astype(BF16),
        "ab_conv": ab_conv_qkv[0], "sc_conv": sc_conv[0],
        "alog": pad_lane(dn_A_log), "dtb": pad_lane(dn_dt_bias),
        "dng": jnp.tile(dn_norm_g[0], heads).reshape(1, width),
        "ab_out": ab_w_out[0].astype(BF16),
        "sb_q": w_sb[:, :sbw], "sb_kt": w_sb[:, sbw:2 * sbw].T, "sb_vt": w_sb[:, 2 * sbw:].T,
        "sb_out": sb_w_out[0].astype(BF16),
    }

    c_all = jnp.concatenate([c_prompt, c_sample], axis=0)
    rows = -(-(bp + bs) // BF16_ROWS) * BF16_ROWS
    c_all = jnp.pad(c_all, ((0, rows - (bp + bs)), (0, 0)))
    mod = _ada_modulation(c_all, ada_w.astype(BF16), ada_b)[:, :bp + bs]
    mod = mod.reshape(depth, bp + bs, N_SUB, 3, 1, d)

    def mods_for(lo, hi):
        return [[tuple(mod[l, lo:hi, s, i] for i in range(3)) for s in range(N_SUB)] for l in range(depth)]

    zeros = lambda shape: jnp.zeros(shape, x_prompt.dtype)
    out_p = _trunk(x_prompt, mods_for(0, bp), zeros((bp, heads, DN_DK, DN_DK)),
                   zeros((bp, DN_CONV - 1, 3 * width)), zeros((bp, SC_CONV - 1, scw)), None, w)
    cache_t = (jnp.swapaxes(cache_k[0], -1, -2), jnp.swapaxes(cache_v[0], -1, -2))
    out_s = _trunk(x_sample, mods_for(bp, bp + bs), state_delta[0], state_qkv_conv[0], state_sconv[0], cache_t, w)
    return (out_p[0], out_s[0]) + out_p[1:] + out_s[1:]
```

```python
import functools
import math

import jax
import jax.numpy as jnp
from jax import lax
from jax.experimental import pallas as pl
from jax.experimental.pallas import tpu as pltpu

F32 = jnp.float32
BF16 = jnp.bfloat16

NORM_EPS = 1e-6
L2_EPS = 1e-6
LOG2_E = math.log2(math.e)
N_SUB = 3
CHUNK = 64
DN_DK = 64
DN_CONV = 4
PRECISE_LEVELS = 5
SC_CONV = 3
SB_DH = 64
ROW_TILE = 512
FF_TILE = 256
FFN_HALVES = 2
SB_BLOCK = 256
SB_CHAINS = 8
SB_CACHE_GROUP = 1
SB_DEAD_LOG2 = -160.0
SUBLANES = 8
BF16_ROWS = 16
LANES = 128
VMEM_LIMIT = 56 * 1024 * 1024


def _params(*semantics):
    return pltpu.CompilerParams(dimension_semantics=semantics, vmem_limit_bytes=VMEM_LIMIT)


def _resident(shape):
    zeros = (0,) * len(shape)
    return pl.BlockSpec(shape, lambda *_: zeros, pipeline_mode=pl.Buffered(1))


def _row_tiling(nseq, t):
    if t >= ROW_TILE:
        assert t % ROW_TILE == 0
        return 1, ROW_TILE
    g = min(nseq, ROW_TILE // t)
    assert nseq % g == 0
    return g, t


def _dot(a, b):
    return jnp.dot(a, b, preferred_element_type=F32)


def _dot_nt(a, b):
    return lax.dot_general(a, b, (((1,), (1,)), ((), ())), preferred_element_type=F32)


def _dot_tn(a, b):
    return lax.dot_general(a, b, (((0,), (0,)), ((), ())), preferred_element_type=F32)


def _split(x):
    hi = x.astype(BF16)
    lo = (x - hi.astype(F32)).astype(BF16)
    return hi, lo


def _dot_x2(x, m_bf16):
    hi, lo = _split(x)
    return _dot(hi, m_bf16) + _dot(lo, m_bf16)


def _split3(x):
    hi = x.astype(BF16)
    r = x - hi.astype(F32)
    mid = r.astype(BF16)
    lo = (r - mid.astype(F32)).astype(BF16)
    return hi, mid, lo


def _dot_x3(x, m_bf16):
    hi, mid, lo = _split3(x)
    return _dot(hi, m_bf16) + (_dot(mid, m_bf16) + _dot(lo, m_bf16))


def _mdot_x3(m_bf16, x):
    hi, mid, lo = _split3(x)
    return _dot(m_bf16, hi) + (_dot(m_bf16, mid) + _dot(m_bf16, lo))


def _dot_p_xp(p, x, with_p):
    ph = p.astype(BF16)
    phf = ph.astype(F32)
    plf = p - phf
    xh, xl = _split(x)
    lhs = jnp.concatenate([phf, plf, phf], axis=1).astype(BF16)
    if with_p:
        yh = jnp.concatenate([xh, ph], axis=1)
        yl = jnp.concatenate([xl, plf.astype(BF16)], axis=1)
    else:
        yh, yl = xh, xl
    return _dot(lhs, jnp.concatenate([yh, yh, yl], axis=0))


def _sigmoid(x):
    return 1.0 / (1.0 + jnp.exp(-x))


def _silu(x):
    return x * _sigmoid(x)


def _softplus(x):
    return jnp.maximum(x, 0.0) + jnp.log(1.0 + jnp.exp(-jnp.abs(x)))


def _rmsnorm(x, g):
    ms = jnp.mean(x * x, axis=-1, keepdims=True)
    return x * lax.rsqrt(ms + NORM_EPS) * g


def _modnorm(x, g, shift, scale):
    return _rmsnorm(x, g) * (1.0 + scale) + shift


def _ada_kernel(c_ref, w_ref, b_ref, o_ref):
    cond = _silu(c_ref[...]).astype(BF16)
    o_ref[0] = _dot(cond, w_ref[0]) + b_ref[0]


def _ada_modulation(c_all, ada_w, ada_b):
    depth, d, n = ada_w.shape
    r = c_all.shape[0]
    tn = 1024 if n % 1024 == 0 else n
    return pl.pallas_call(
        _ada_kernel,
        grid=(depth, n // tn),
        in_specs=[
            pl.BlockSpec((r, d), lambda l, j: (0, 0)),
            pl.BlockSpec((1, d, tn), lambda l, j: (l, 0, j)),
            pl.BlockSpec((1, 1, tn), lambda l, j: (l, 0, j)),
        ],
        out_specs=pl.BlockSpec((1, r, tn), lambda l, j: (l, 0, j)),
        out_shape=jax.ShapeDtypeStruct((depth, r, n), F32),
        compiler_params=_params("parallel", "parallel"),
        name="ada_modulation",
    )(c_all, ada_w, ada_b.reshape(depth, 1, n))


def _ffn_kernel(x_ref, sh_ref, sc_ref, gt_ref, g_ref, win_ref, wout_ref, *rest, mixer, final):
    o_ref = rest[-1]
    bg, bt, d = x_ref.shape
    f = wout_ref.shape[0]
    if mixer:
        a_ref, mg_ref, wo_ref = rest[:3]
        rest = rest[3:]
    halves = FFN_HALVES if (bg % FFN_HALVES == 0 or (bg == 1 and bt % (FFN_HALVES * BF16_ROWS) == 0)) else 1
    for part in range(halves):
        if bg % halves == 0:
            sel = (slice(part * bg // halves, (part + 1) * bg // halves), slice(None))
        else:
            sel = (slice(None), slice(part * bt // halves, (part + 1) * bt // halves))
        x = x_ref[sel]
        gg, tt, _ = x.shape
        rows = (sel[0], slice(None))
        if mixer:
            mix = _dot(a_ref[sel].reshape(gg * tt, a_ref.shape[2]), wo_ref[...])
            x = x + mg_ref[rows] * mix.reshape(gg, tt, d)
        h = _modnorm(x, g_ref[...], sh_ref[rows], sc_ref[rows]).astype(BF16).reshape(gg * tt, d)
        acc = jnp.zeros((gg * tt, d), F32)
        for j in range(f // FF_TILE):
            lo = j * FF_TILE
            gate = _dot(h, win_ref[:, lo:lo + FF_TILE])
            up = _dot(h, win_ref[:, f + lo:f + lo + FF_TILE])
            act = (_silu(gate) * up).astype(BF16)
            acc = acc + _dot(act, wout_ref[lo:lo + FF_TILE, :])
        y = x + 0.5 * gt_ref[rows] * acc.reshape(gg, tt, d)
        if final:
            y = _rmsnorm(y, rest[0][...])
        o_ref[sel] = y


def _ffn(x, mod, g, w_in, w_out, mixer=None, final_g=None):
    nseq, t, d = x.shape
    f = w_out.shape[0]
    assert f % FF_TILE == 0
    gg, tt = _row_tiling(nseq, t)
    shift, scale, gate = mod
    xspec = pl.BlockSpec((gg, tt, d), lambda i, j: (i, j, 0))
    mspec = pl.BlockSpec((gg, 1, d), lambda i, j: (i, 0, 0))
    in_specs = [xspec, mspec, mspec, mspec, _resident((1, d)), _resident((d, 2 * f)), _resident((f, d))]
    args = [x, shift, scale, gate, g.reshape(1, d), w_in, w_out]
    if mixer is not None:
        a, mix_gate, w_o = mixer
        in_specs += [pl.BlockSpec((gg, tt, a.shape[2]), lambda i, j: (i, j, 0)), mspec, _resident(w_o.shape)]
        args += [a, mix_gate, w_o]
    if final_g is not None:
        in_specs.append(_resident((1, d)))
        args.append(final_g.reshape(1, d))
    return pl.pallas_call(
        functools.partial(_ffn_kernel, mixer=mixer is not None, final=final_g is not None),
        grid=(nseq // gg, t // tt),
        in_specs=in_specs,
        out_specs=xspec,
        out_shape=jax.ShapeDtypeStruct(x.shape, F32),
        compiler_params=_params("parallel", "parallel"),
        name="ffn_final" if final_g is not None else "ffn",
    )(*args)


def _proj_kernel(x_ref, sh_ref, sc_ref, g_ref, *refs):
    n_out = len(refs) // 2
    x = x_ref[...]
    gg, tt, d = x.shape
    h = _modnorm(x, g_ref[...], sh_ref[...], sc_ref[...]).astype(BF16).reshape(gg * tt, d)
    for w_ref, o_ref in zip(refs[:n_out], refs[n_out:]):
        o_ref[...] = _dot(h, w_ref[...]).reshape(o_ref.shape).astype(o_ref.dtype)


def _norm_proj(x, mod, g, weights, name):
    nseq, t, d = x.shape
    gg, tt = _row_tiling(nseq, t)
    shift, scale, _ = mod
    xspec = pl.BlockSpec((gg, tt, d), lambda i, j: (i, j, 0))
    mspec = pl.BlockSpec((gg, 1, d), lambda i, j: (i, 0, 0))
    return pl.pallas_call(
        _proj_kernel,
        grid=(nseq // gg, t // tt),
        in_specs=[xspec, mspec, mspec, _resident((1, d))] + [_resident(w.shape) for w in weights],
        out_specs=[pl.BlockSpec((gg, tt, w.shape[1]), lambda i, j: (i, j, 0)) for w in weights],
        out_shape=[jax.ShapeDtypeStruct((nseq, t, w.shape[1]), F32) for w in weights],
        compiler_params=_params("parallel", "parallel"),
        name=name,
    )(x, shift, scale, g.reshape(1, d), *weights)


def _sb_qkv_kernel(x_ref, sh_ref, sc_ref, g_ref, wq_ref, wkt_ref, wvt_ref,
                   q_ref, kt_ref, vt_ref, ktb_ref, vtb_ref):
    x = x_ref[...]
    gg, tt, d = x.shape
    heads = kt_ref.shape[1]
    h = _modnorm(x, g_ref[...], sh_ref[...], sc_ref[...]).astype(BF16).reshape(gg * tt, d)
    q = _dot(h, wq_ref[...]) * (SB_DH ** -0.5 * LOG2_E)
    q_ref[...] = q.reshape(q_ref.shape).astype(q_ref.dtype)
    for w_ref, o_ref, b_ref in ((wkt_ref, kt_ref, ktb_ref), (wvt_ref, vt_ref, vtb_ref)):
        yt = _dot_nt(w_ref[...], h)
        for s in range(gg):
            y = yt[:, s * tt:(s + 1) * tt].reshape(heads, SB_DH, tt)
            o_ref[s] = y
            b_ref[s] = y.astype(b_ref.dtype)


def _sb_qkv(x, mod, g, w_q, w_kt, w_vt):
    nseq, t, d = x.shape
    width = w_q.shape[1]
    heads = width // SB_DH
    gg, tt = _row_tiling(nseq, t)
    shift, scale, _ = mod
    xspec = pl.BlockSpec((gg, tt, d), lambda i, j: (i, j, 0))
    mspec = pl.BlockSpec((gg, 1, d), lambda i, j: (i, 0, 0))
    tspec = pl.BlockSpec((gg, heads, SB_DH, tt), lambda i, j: (i, 0, 0, j))
    tshape = jax.ShapeDtypeStruct((nseq, heads, SB_DH, t), F32)
    return pl.pallas_call(
        _sb_qkv_kernel,
        grid=(nseq // gg, t // tt),
        in_specs=[xspec, mspec, mspec, _resident((1, d)),
                  _resident(w_q.shape), _resident(w_kt.shape), _resident(w_vt.shape)],
        out_specs=[pl.BlockSpec((gg, tt, width), lambda i, j: (i, j, 0)), tspec, tspec, tspec, tspec],
        out_shape=[jax.ShapeDtypeStruct((nseq, t, width), BF16), tshape, tshape,
                   jax.ShapeDtypeStruct(tshape.shape, BF16), jax.ShapeDtypeStruct(tshape.shape, BF16)],
        compiler_params=_params("parallel", "parallel"),
        name="sb_qkv",
    )(x, shift, scale, g.reshape(1, d), w_q, w_kt, w_vt)


def _causal_conv(x, prev8, w):
    taps = w.shape[0]
    full = jnp.concatenate([prev8, x], axis=0)
    y = x * w[taps - 1:taps]
    for j in range(taps - 1):
        y = y + pltpu.roll(full, taps - 1 - j, 0)[SUBLANES:] * w[j:j + 1]
    return y


def _gdn_patterns(rows, heads):
    width = heads * DN_DK
    iota = lambda shape, dim: lax.broadcasted_iota(jnp.int32, shape, dim)
    head_ones = iota((width, width), 0) // DN_DK == iota((width, width), 1) // DN_DK
    rr, cc = iota((rows, rows), 0), iota((rows, rows), 1)
    chunk_ones = rr // CHUNK == cc // CHUNK
    chunk_tril = jnp.logical_and(chunk_ones, cc <= rr)
    li, lh = iota((LANES, width), 0), iota((LANES, width), 1) // DN_DK
    return [m.astype(BF16) for m in (head_ones, chunk_ones, chunk_tril, li == lh, li == lh + heads)]


def _gdn_kernel(qkv_ref, z_ref, ab_ref, s_ref, s0_ref, conv0_ref, sc0_ref,
                wconv_ref, alog_ref, dtb_ref, dng_ref, wsc_ref,
                head_ones_ref, chunk_ones_ref, chunk_tril_ref, expand_g_ref, expand_b_ref,
                y_ref, sout_ref, convout_ref, scout_ref,
                state_ref, convprev_ref, scprev_ref, o_scr):
    step = pl.program_id(1)
    gg, tt, _ = qkv_ref.shape
    heads = state_ref.shape[1]
    width = heads * DN_DK
    scw = wsc_ref.shape[1]
    nc = tt // CHUNK
    nb = gg * nc
    rows = gg * tt

    @pl.when(step == 0)
    def _():
        state_ref[...] = s0_ref[...]
        convprev_ref[...] = conv0_ref[...]
        scprev_ref[...] = sc0_ref[...]

    def per_seq(fn):
        parts = [fn(s) for s in range(gg)]
        return parts[0] if gg == 1 else jnp.concatenate(parts, axis=0)

    wc = wconv_ref[...]
    wsc = wsc_ref[...]
    qkv = _silu(per_seq(lambda s: _causal_conv(qkv_ref[s], convprev_ref[s], wc)))
    cx_parts = [s_ref[s, :, scw:2 * scw] * s_ref[s, :, 2 * scw:] for s in range(gg)]
    y_sc = per_seq(lambda s: s_ref[s, :, :scw] * _causal_conv(cx_parts[s], scprev_ref[s], wsc))
    for s in range(gg):
        convprev_ref[s] = qkv_ref[s, tt - SUBLANES:, :]
        scprev_ref[s] = cx_parts[s][tt - SUBLANES:]
    q, k, v = qkv[:, :width], qkv[:, width:2 * width], qkv[:, 2 * width:]

    head_ones = head_ones_ref[...]
    chunk_ones = chunk_ones_ref[...]
    chunk_tril = chunk_tril_ref[...]
    expand_g = expand_g_ref[...]
    expand_b = expand_b_ref[...]
    fr = lax.broadcasted_iota(jnp.int32, (rows, width), 0) % CHUNK
    ss = lax.broadcasted_iota(jnp.int32, (rows, width), 1) % DN_DK
    incl_all = ss <= fr
    eye_all = ss == fr
    t2 = lax.broadcasted_iota(jnp.int32, (CHUNK, CHUNK), 0)
    s2 = lax.broadcasted_iota(jnp.int32, (CHUNK, CHUNK), 1)
    strict = s2 < t2

    q = q * lax.rsqrt(_dot_x2(q * q, head_ones) + L2_EPS) * (DN_DK ** -0.5)
    k = k * lax.rsqrt(_dot_x2(k * k, head_ones) + L2_EPS)

    ab = per_seq(lambda s: ab_ref[s])
    g_log = -jnp.exp(alog_ref[...]) * _softplus(ab + dtb_ref[...])
    g_cum = _mdot_x3(chunk_tril, g_log)
    gexp = _dot_x3(g_cum, expand_g)
    beta = _dot_x2(_sigmoid(ab), expand_b)
    grow = _mdot_x3(chunk_ones, jnp.where(eye_all, gexp, 0.0))
    decay = jnp.where(incl_all, jnp.exp(jnp.where(incl_all, gexp - grow, 0.0)), 0.0)
    eg = jnp.exp(gexp)
    g3 = gexp.reshape(nb, CHUNK, width)
    g_end = jnp.broadcast_to(g3[:, CHUNK - 1:CHUNK, :], g3.shape).reshape(rows, width)
    k_dec = (k * jnp.exp(g_end - gexp)).astype(BF16)
    eg_end = jnp.exp(g_end)
    bv = beta * v
    bk = beta * eg * k
    qg = eg * q
    q_bf = q.astype(BF16)
    k_bf = k.astype(BF16)

    def blk(a, n, h):
        return a[n * CHUNK:(n + 1) * CHUNK, h * DN_DK:(h + 1) * DN_DK]

    chains = [(n, h) for n in range(nb) for h in range(heads)]
    kk = {c: _dot_nt(blk(k_bf, *c), blk(k_bf, *c)) for c in chains}
    p = {c: -jnp.where(strict, blk(beta, *c) * kk[c] * blk(decay, *c), 0.0) for c in chains}
    x = {c: jnp.concatenate([blk(bv, *c), blk(bk, *c)], axis=1) for c in chains}
    levels = CHUNK.bit_length() - 1
    for lvl in range(levels):
        last = lvl == levels - 1
        res = {}
        for c in chains:
            if lvl < PRECISE_LEVELS:
                res[c] = _dot_p_xp(p[c], x[c], not last)
            else:
                rhs = x[c] if last else jnp.concatenate([x[c], p[c]], axis=1)
                res[c] = _dot(p[c].astype(BF16), rhs.astype(BF16))
        for c in chains:
            x[c] = x[c] + res[c][:, :2 * DN_DK]
            if not last:
                p[c] = res[c][:, 2 * DN_DK:]
    x_bf = {c: x[c].astype(BF16) for c in chains}
    cn = {c: _dot_tn(blk(k_dec, *c), x_bf[c]) for c in chains}
    qk = {c: (_dot_nt(blk(q_bf, *c), blk(k_bf, *c)) * blk(decay, *c)).astype(BF16) for c in chains}
    rp = {c: _dot(qk[c], x_bf[c]) for c in chains}
    pc = {c: jnp.concatenate([blk(qg, *c) - rp[c][:, DN_DK:], cn[c][:, DN_DK:]], axis=0).astype(BF16)
          for c in chains}
    for s in range(gg):
        state = [state_ref[s, h] for h in range(heads)]
        for i in range(nc):
            n = s * nc + i
            ps = [_dot(pc[(n, h)], state[h].astype(BF16)) for h in range(heads)]
            for h in range(heads):
                o_scr[n * CHUNK:(n + 1) * CHUNK, h * DN_DK:(h + 1) * DN_DK] = ps[h][:CHUNK] + rp[(n, h)][:, :DN_DK]
                state[h] = blk(eg_end, n, h) * state[h] - ps[h][CHUNK:] + cn[(n, h)][:, :DN_DK]
        for h in range(heads):
            state_ref[s, h] = state[h]

    o_all = o_scr[...]
    ms = _dot_x2(o_all * o_all, head_ones) * (1.0 / DN_DK)
    z = per_seq(lambda s: z_ref[s])
    o_all = o_all * lax.rsqrt(ms + NORM_EPS) * dng_ref[...] * _silu(z)
    y_ref[:, :, :width] = o_all.reshape(gg, tt, width).astype(y_ref.dtype)
    y_ref[:, :, width:] = y_sc.reshape(gg, tt, scw).astype(y_ref.dtype)

    @pl.when(step == pl.num_programs(1) - 1)
    def _():
        sout_ref[...] = state_ref[...]
        convout_ref[...] = convprev_ref[...]
        scout_ref[...] = scprev_ref[...]


def _gdn(qkv, z, ab, s, s0, conv0, sc0, wconv, alog, dtb, dng, wsc):
    nseq, t, w3 = qkv.shape
    heads = s0.shape[1]
    width = heads * DN_DK
    scw = wsc.shape[1]
    gg, tt = _row_tiling(nseq, t)
    assert tt % CHUNK == 0
    row = lambda n: pl.BlockSpec((gg, tt, n), lambda i, j: (i, j, 0))
    per_seq = lambda shp: pl.BlockSpec((gg,) + shp, lambda i, j: (i,) + (0,) * len(shp))
    patterns = _gdn_patterns(gg * tt, heads)
    return pl.pallas_call(
        _gdn_kernel,
        grid=(nseq // gg, t // tt),
        in_specs=[row(w3), row(width), row(LANES), row(3 * scw),
                  per_seq((heads, DN_DK, DN_DK)), per_seq((SUBLANES, w3)), per_seq((SUBLANES, scw)),
                  _resident(wconv.shape), _resident((1, LANES)), _resident((1, LANES)),
                  _resident((1, width)), _resident(wsc.shape)] + [_resident(m.shape) for m in patterns],
        out_specs=[row(width + scw), per_seq((heads, DN_DK, DN_DK)),
                   per_seq((SUBLANES, w3)), per_seq((SUBLANES, scw))],
        out_shape=[jax.ShapeDtypeStruct((nseq, t, width + scw), BF16),
                   jax.ShapeDtypeStruct((nseq, heads, DN_DK, DN_DK), F32),
                   jax.ShapeDtypeStruct((nseq, SUBLANES, w3), F32),
                   jax.ShapeDtypeStruct((nseq, SUBLANES, scw), F32)],
        scratch_shapes=[pltpu.VMEM((gg, heads, DN_DK, DN_DK), F32),
                        pltpu.VMEM((gg, SUBLANES, w3), F32),
                        pltpu.VMEM((gg, SUBLANES, scw), F32),
                        pltpu.VMEM((gg * tt, width), F32)],
        compiler_params=_params("parallel", "arbitrary"),
        name="gdn_sconv",
    )(qkv, z, ab, s, s0, conv0, sc0, wconv, alog, dtb, dng, wsc, *patterns)


def _sb_blocks(items, upper):
    def run(carries, scale_out):
        n = len(items)
        z, log_sig, log_rest, later, w, scale = ([None] * n for _ in range(6))
        accs, carry_now = {}, dict(carries)

        def stage(i, s):
            chain, q, kt, vt, mask = items[i]
            if s == 0:
                z[i] = _dot(q, kt)
            elif s == 1:
                soft = jnp.log2(1.0 + jnp.exp2(-jnp.abs(z[i])))
                log_sig[i] = jnp.minimum(z[i], 0.0) - soft
                lr = log_sig[i] - z[i]
                log_rest[i] = lr if mask is None else jnp.where(mask, lr, 0.0)
            elif s == 2:
                tk = log_rest[i].shape[1]
                later[i] = _dot(log_rest[i].astype(BF16), upper[:tk, :tk])
            elif s == 3:
                carry = carry_now[chain]
                arg = log_sig[i] + later[i]
                wi = jnp.exp2(arg if scale_out else arg + carry)
                w[i] = (wi if mask is None else jnp.where(mask, wi, 0.0)).astype(BF16)
                scale[i] = jnp.exp2(carry) if scale_out else None
                carry_now[chain] = carry + (later[i][:, 0:1] + log_rest[i][:, 0:1])
            else:
                o = _dot_nt(w[i], vt)
                if scale[i] is not None:
                    o = o * scale[i]
                accs[chain] = o if chain not in accs else accs[chain] + o

        stages = 5
        for tick in range(n + stages - 1):
            for i in range(n):
                if 0 <= tick - i < stages:
                    stage(i, tick - i)
        return {chain: (accs[chain], carry_now[chain]) for chain in accs}
    return run


def _upper(n):
    j = lax.broadcasted_iota(jnp.int32, (n, n), 0)
    s = lax.broadcasted_iota(jnp.int32, (n, n), 1)
    return (j > s).astype(BF16)


def _any_alive(carries):
    top = functools.reduce(jnp.maximum, [jnp.max(c) for c in carries])
    return (top > SB_DEAD_LOG2).astype(jnp.int32)


def _sb_prompt_kernel(q_ref, kt_ref, vt_ref, o_ref):
    hp = kt_ref.shape[1]
    t = q_ref.shape[1]
    blk = min(SB_BLOCK, t)
    upper = _upper(blk)
    tq = lax.broadcasted_iota(jnp.int32, (blk, blk), 0)
    ts = lax.broadcasted_iota(jnp.int32, (blk, blk), 1)
    causal = ts < tq
    zero = jnp.zeros((blk, 1), F32)

    def q_block(qb, has_prev):
        q0 = qb * blk if isinstance(qb, int) else pl.multiple_of(qb * blk, blk)
        qs = [q_ref[0, pl.ds(q0, blk), hh * SB_DH:(hh + 1) * SB_DH] for hh in range(hp)]

        def items(k0, mask):
            return [(hh, qs[hh], kt_ref[0, hh, :, pl.ds(k0, blk)], vt_ref[0, hh, :, pl.ds(k0, blk)], mask)
                    for hh in range(hp)]

        its = items(q0, causal)
        if has_prev:
            its = its + items(pl.multiple_of(q0 - blk, blk), None)
        first = _sb_blocks(its, upper)({hh: zero for hh in range(hp)}, True)
        done = 2 if has_prev else 1

        def k_block(st):
            i, _, oc = st
            k0 = pl.multiple_of((qb - i) * blk, blk)
            res = _sb_blocks(items(k0, None), upper)({hh: oc[hh][1] for hh in range(hp)}, True)
            new = tuple((oc[hh][0] + res[hh][0], res[hh][1]) for hh in range(hp))
            return i + 1, _any_alive([c for _, c in new]), new

        start = tuple(first[hh] for hh in range(hp))
        out = lax.while_loop(lambda st: jnp.logical_and(st[0] <= qb, st[1] > 0), k_block,
                             (jnp.int32(done), _any_alive([c for _, c in start]), start))[2]
        o = jnp.concatenate([out[hh][0] for hh in range(hp)], axis=1)
        o_ref[0, pl.ds(q0, blk), :] = o.astype(o_ref.dtype)

    q_block(0, False)

    def later_q_block(qb, _):
        q_block(qb, True)
        return 0

    lax.fori_loop(1, t // blk, later_q_block, 0)


def _sb_prompt(q, kt, vt):
    b, heads, dh, t = kt.shape
    hp = min(SB_CHAINS, heads)
    assert t % min(SB_BLOCK, t) == 0 and heads % hp == 0 and (hp * dh) % LANES == 0
    qspec = pl.BlockSpec((1, t, hp * dh), lambda i, j: (i, 0, j))
    tspec = pl.BlockSpec((1, hp, dh, t), lambda i, j: (i, j, 0, 0))
    return pl.pallas_call(
        _sb_prompt_kernel,
        grid=(b, heads // hp),
        in_specs=[qspec, tspec, tspec],
        out_specs=qspec,
        out_shape=jax.ShapeDtypeStruct((b, t, heads * dh), BF16),
        compiler_params=_params("parallel", "parallel"),
        name="sb_prompt",
    )(q, kt, vt)


def _sb_sample_kernel(q_ref, kt_ref, vt_ref, ckt_hbm, cvt_hbm, o_ref, kbuf, vbuf, sem):
    hs = kt_ref.shape[1]
    t = q_ref.shape[1]
    past = ckt_hbm.shape[3]
    blk = min(SB_BLOCK, past)
    rows = hs * t
    upper = _upper(max(blk, t))
    nblk = past // blk
    group = math.gcd(nblk, SB_CACHE_GROUP)
    ngroups = nblk // group
    nj = pl.num_programs(1)
    step = pl.program_id(0) * nj + pl.program_id(1)
    older_slot = 2

    def cache_copies(at_step, g, slot):
        k0 = pl.multiple_of(past - (g + 1) * group * blk, blk)
        window = (at_step // nj, pl.ds((at_step % nj) * hs, hs), slice(None), pl.ds(k0, group * blk))
        return (pltpu.make_async_copy(ckt_hbm.at[window], kbuf.at[slot], sem.at[0, slot]),
                pltpu.make_async_copy(cvt_hbm.at[window], vbuf.at[slot], sem.at[1, slot]))

    def start_group(at_step, g, slot):
        for cp in cache_copies(at_step, g, slot):
            cp.start()

    def wait_group(at_step, g, slot):
        for cp in cache_copies(at_step, g, slot):
            cp.wait()

    @pl.when(step == 0)
    def _():
        start_group(step, 0, 0)

    @pl.when(step + 1 < pl.num_programs(0) * nj)
    def _():
        start_group(step + 1, 0, (step + 1) % 2)

    rh = lax.broadcasted_iota(jnp.int32, (rows, hs * SB_DH), 0) // t
    ch = lax.broadcasted_iota(jnp.int32, (rows, hs * SB_DH), 1) // SB_DH
    own = rh == ch
    q4 = q_ref[0]
    q_bd = jnp.where(own, jnp.concatenate([q4] * hs, axis=0), jnp.zeros((), q4.dtype))
    fq = lax.broadcasted_iota(jnp.int32, (rows, t), 0) % t
    fs = lax.broadcasted_iota(jnp.int32, (rows, t), 1)
    causal = fs < fq

    new = (0, q_bd, kt_ref[0].reshape(hs * SB_DH, t), vt_ref[0].reshape(hs * SB_DH, t), causal)

    def cache_items(slot):
        lanes = [slice(r * blk, (r + 1) * blk) for r in reversed(range(group))]
        return [(0, q_bd, kbuf[slot, :, :, ln].reshape(hs * SB_DH, blk).astype(BF16),
                 vbuf[slot, :, :, ln].reshape(hs * SB_DH, blk).astype(BF16), None) for ln in lanes]

    wait_group(step, 0, step % 2)
    acc = _sb_blocks([new] + cache_items(step % 2), upper)({0: jnp.zeros((rows, 1), F32)}, False)[0]

    def older_group(st):
        g, _, oc = st
        start_group(step, g, older_slot)
        wait_group(step, g, older_slot)
        res = _sb_blocks(cache_items(older_slot), upper)({0: oc[1]}, False)[0]
        return g + 1, _any_alive([res[1]]), (oc[0] + res[0], res[1])

    o_full, _ = lax.while_loop(lambda st: jnp.logical_and(st[0] < ngroups, st[1] > 0), older_group,
                               (jnp.int32(1), _any_alive([acc[1]]), acc))[2]

    o_own = jnp.where(own, o_full, 0.0)
    o = o_own[:t]
    for i in range(1, hs):
        o = o + o_own[i * t:(i + 1) * t]
    o_ref[0] = o.astype(o_ref.dtype)


def _sb_sample(q, kt, vt, ckt, cvt):
    b, heads, dh, t = kt.shape
    past = ckt.shape[3]
    hs = SB_BLOCK // t
    blk = min(SB_BLOCK, past)
    assert heads % hs == 0 and past % blk == 0
    group = math.gcd(past // blk, SB_CACHE_GROUP)
    qspec = pl.BlockSpec((1, t, hs * dh), lambda i, j: (i, 0, j))
    tspec = pl.BlockSpec((1, hs, dh, t), lambda i, j: (i, j, 0, 0))
    hbm = pl.BlockSpec(memory_space=pl.ANY)
    slots = 3
    window = pltpu.VMEM((slots, hs, dh, group * blk), ckt.dtype)
    return pl.pallas_call(
        _sb_sample_kernel,
        grid=(b, heads // hs),
        in_specs=[qspec, tspec, tspec, hbm, hbm],
        out_specs=qspec,
        out_shape=jax.ShapeDtypeStruct((b, t, heads * dh), BF16),
        scratch_shapes=[window, window, pltpu.SemaphoreType.DMA((2, slots))],
        compiler_params=_params("arbitrary", "arbitrary"),
        name="sb_sample",
    )(q, kt, vt, ckt, cvt)


def _pad_rows(state, rows):
    return jnp.pad(state, ((0, 0), (rows - state.shape[1], 0), (0, 0)))


def _trunk(x, mods, s_delta, s_qkv, s_sc, cache_t, w):
    x = _ffn(x, mods[0][0], w["norm_g"][0, 0], w["ff_in"][0, 0], w["ff_out"][0, 0])
    qkv, z, ab, s = _norm_proj(x, mods[0][1], w["norm_g"][0, 1],
                               [w["ab_qkv"], w["ab_z"], w["ab_ab"], w["ab_s"]], "ab_in_proj")
    y, new_delta, conv8, sc8 = _gdn(qkv, z, ab, s, s_delta, _pad_rows(s_qkv, SUBLANES),
                                    _pad_rows(s_sc, SUBLANES), w["ab_conv"], w["alog"], w["dtb"],
                                    w["dng"], w["sc_conv"])
    x = _ffn(x, mods[0][2], w["norm_g"][0, 2], w["ff_in"][0, 1], w["ff_out"][0, 1],
             mixer=(y, mods[0][1][2], w["ab_out"]))
    x = _ffn(x, mods[1][0], w["norm_g"][1, 0], w["ff_in"][1, 0], w["ff_out"][1, 0])
    q, kt, vt, ktb, vtb = _sb_qkv(x, mods[1][1], w["norm_g"][1, 1], w["sb_q"], w["sb_kt"], w["sb_vt"])
    o = _sb_prompt(q, ktb, vtb) if cache_t is None else _sb_sample(q, ktb, vtb, cache_t[0], cache_t[1])
    y_out = _ffn(x, mods[1][2], w["norm_g"][1, 2], w["ff_in"][1, 1], w["ff_out"][1, 1],
                 mixer=(o, mods[1][1][2], w["sb_out"]), final_g=w["final_g"])
    new_qkv = conv8[:, SUBLANES - (DN_CONV - 1):]
    new_sc = sc8[:, SUBLANES - (SC_CONV - 1):]
    new_k = jnp.swapaxes(kt, -1, -2)
    new_v = jnp.swapaxes(vt, -1, -2)
    return y_out, new_delta[None], new_qkv[None], new_sc[None], new_k[None], new_v[None]


def kernel(x_prompt, x_sample, c_prompt, c_sample, state_delta, state_qkv_conv, state_sconv, cache_k, cache_v,
           norm_g, ada_w, ada_b, ff_w_in, ff_w_out, ab_w_in, ab_conv_qkv, dn_A_log, dn_dt_bias, dn_norm_g,
           sc_conv, ab_w_out, sb_w_qkv, sb_w_out, final_g):
    depth, _, d = norm_g.shape
    assert depth == 2 and ab_w_in.shape[0] == 1 and sb_w_qkv.shape[0] == 1
    bp, bs = x_prompt.shape[0], x_sample.shape[0]
    heads = dn_A_log.shape[1]
    width = heads * DN_DK
    scw = sc_conv.shape[2]
    sbw = sb_w_qkv.shape[2] // 3
    assert 2 * heads <= LANES

    o_z, o_a, o_s = 3 * width, 4 * width, 4 * width + 2 * heads
    w_in = ab_w_in[0]
    w_sb = sb_w_qkv[0].astype(BF16)
    pad_lane = lambda a: jnp.pad(a, ((0, 0), (0, LANES - a.shape[1])))
    w = {
        "norm_g": norm_g, "final_g": final_g,
        "ff_in": ff_w_in.astype(BF16), "ff_out": ff_w_out.astype(BF16),
        "ab_qkv": w_in[:, :o_z].astype(BF16), "ab_z": w_in[:, o_z:o_a].astype(BF16),
        "ab_ab": pad_lane(w_in[:, o_a:o_s]).astype(BF16), "ab_s": w_in[:, o_s:].astype(BF16),
        "ab_conv": ab_conv_qkv[0], "sc_conv": sc_conv[0],
        "alog": pad_lane(dn_A_log), "dtb": pad_lane(dn_dt_bias),
        "dng": jnp.tile(dn_norm_g[0], heads).reshape(1, width),
        "ab_out": ab_w_out[0].astype(BF16),
        "sb_q": w_sb[:, :sbw], "sb_kt": w_sb[:, sbw:2 * sbw].T, "sb_vt": w_sb[:, 2 * sbw:].T,
        "sb_out": sb_w_out[0].astype(BF16),
    }

    c_all = jnp.concatenate([c_prompt, c_sample], axis=0)
    rows = -(-(bp + bs) // BF16_ROWS) * BF16_ROWS
    c_all = jnp.pad(c_all, ((0, rows - (bp + bs)), (0, 0)))
    mod = _ada_modulation(c_all, ada_w.astype(BF16), ada_b)[:, :bp + bs]
    mod = mod.reshape(depth, bp + bs, N_SUB, 3, 1, d)

    def mods_for(lo, hi):
        return [[tuple(mod[l, lo:hi, s, i] for i in range(3)) for s in range(N_SUB)] for l in range(depth)]

    zeros = lambda shape: jnp.zeros(shape, x_prompt.dtype)
    out_p = _trunk(x_prompt, mods_for(0, bp), zeros((bp, heads, DN_DK, DN_DK)),
                   zeros((bp, DN_CONV - 1, 3 * width)), zeros((bp, SC_CONV - 1, scw)), None, w)
    cache_t = (jnp.swapaxes(cache_k[0], -1, -2), jnp.swapaxes(cache_v[0], -1, -2))
    out_s = _trunk(x_sample, mods_for(bp, bp + bs), state_delta[0], state_qkv_conv[0], state_sconv[0], cache_t, w)
    return (out_p[0], out_s[0]) + out_p[1:] + out_s[1:]
```
